```python
import jax, jax.numpy as jnp
from jax import lax
import numpy as np

D_MODEL = 1024
BATCH = 8
SEQ = 2048
DEPTH = 4

N_MIXERS = 4
DEEPNORM_ALPHA = (2.0 * DEPTH) ** 0.25
DEEPNORM_BETA = (8.0 * DEPTH) ** -0.25
LN_EPS = 1e-5
RMS_EPS = 1e-6
Q_BLOCK = 128

A_CHUNK = 128
A_HALF = 2 * D_MODEL
A_GROUPS = 8
A_GROUP_DIM = A_HALF // A_GROUPS

B_HEADS = 16
B_HEAD_DIM = D_MODEL // B_HEADS
B_IDX_HEADS = 4
B_IDX_DIM = 64
B_TOPK_MAX = 256
B_IN = 3 * B_HEADS * B_HEAD_DIM + B_IDX_HEADS * B_IDX_DIM + B_IDX_DIM + B_IDX_HEADS

C_HEADS = 8
C_EXPAND = 128
C_HEAD_V = D_MODEL // C_HEADS
C_FDIM = C_HEADS * C_EXPAND
C_CHUNK = 64

D_HEADS = 16
D_NOPE = 64
D_ROPE = 32
D_VDIM = 64
D_Q_RANK = 256
D_KV_RANK = 128
ROPE_BASE = 10000.0

FFN_DIM = 3584
N_EXPERTS = 8
TOP_K = 2
MOE_BLOCK = 128

kernel_name = "hybrid_interleaved_gmlp_dsa_hgrn2_mla_moe"

F32 = jnp.float32


def layer_norm(x, g, b):
    xf = x.astype(F32)
    mu = jnp.mean(xf, axis=-1, keepdims=True)
    var = jnp.mean(jnp.square(xf - mu), axis=-1, keepdims=True)
    return ((xf - mu) * lax.rsqrt(var + LN_EPS) * g.astype(F32) + b.astype(F32)).astype(x.dtype)


def rms_norm(x, g):
    xf = x.astype(F32)
    return (xf * lax.rsqrt(jnp.mean(jnp.square(xf), axis=-1, keepdims=True) + RMS_EPS) * g.astype(F32)).astype(x.dtype)


def rope(x, positions):
    half = x.shape[-1] // 2
    inv_freq = ROPE_BASE ** (-jnp.arange(half, dtype=F32) / half)
    ang = positions.astype(F32)[:, :, None] * inv_freq
    cos, sin = jnp.cos(ang)[:, :, None, :], jnp.sin(ang)[:, :, None, :]
    xf = x.astype(F32)
    x1, x2 = xf[..., :half], xf[..., half:]
    return jnp.concatenate([x1 * cos - x2 * sin, x2 * cos + x1 * sin], axis=-1).astype(x.dtype)


def mixer_gmlp(x, w_in, ln_g, ln_b, w_s, b_s, w_out):
    bsz, seq, _ = x.shape
    n_chunks = seq // A_CHUNK
    uv = jax.nn.gelu(x @ w_in)
    u, v = uv[..., :A_HALF], uv[..., A_HALF:]
    v = layer_norm(v, ln_g, ln_b).reshape(bsz, n_chunks, A_CHUNK, A_GROUPS, A_GROUP_DIM)
    causal = jnp.tril(jnp.ones((A_CHUNK, A_CHUNK), dtype=bool))
    w_causal = jnp.where(causal, w_s, 0.0).astype(v.dtype)
    s = jnp.einsum('gts,bnsgd->bntgd', w_causal, v) + b_s.T.astype(v.dtype)[None, None, :, :, None]
    return (u * s.reshape(bsz, seq, A_HALF)) @ w_out


def mixer_dsa(x, w_in, w_out):
    bsz, seq, _ = x.shape
    topk = min(B_TOPK_MAX, seq // 4)
    hd = B_HEADS * B_HEAD_DIM
    proj = x @ w_in
    q = proj[..., :hd].reshape(bsz, seq, B_HEADS, B_HEAD_DIM)
    k = proj[..., hd:2 * hd].reshape(bsz, seq, B_HEADS, B_HEAD_DIM)
    v = proj[..., 2 * hd:3 * hd].reshape(bsz, seq, B_HEADS, B_HEAD_DIM)
    o = 3 * hd
    q_idx = proj[..., o:o + B_IDX_HEADS * B_IDX_DIM].reshape(bsz, seq, B_IDX_HEADS, B_IDX_DIM).astype(F32)
    o += B_IDX_HEADS * B_IDX_DIM
    k_idx = proj[..., o:o + B_IDX_DIM].astype(F32)
    o += B_IDX_DIM
    w_idx = proj[..., o:o + B_IDX_HEADS].astype(F32)
    idx_scale = (B_IDX_DIM * B_IDX_HEADS) ** -0.5
    attn_scale = B_HEAD_DIM ** -0.5
    k_pos = jnp.arange(seq)
    take_rows = jax.vmap(lambda arr, sel: arr[sel])

    def block(i):
        start = i * Q_BLOCK
        q_pos = start + jnp.arange(Q_BLOCK)
        qb = lax.dynamic_slice_in_dim(q, start, Q_BLOCK, axis=1)
        qib = lax.dynamic_slice_in_dim(q_idx, start, Q_BLOCK, axis=1)
        wib = lax.dynamic_slice_in_dim(w_idx, start, Q_BLOCK, axis=1)
        dots = jnp.einsum('bthd,bsd->bths', qib, k_idx)
        score = jnp.einsum('bths,bth->bts', jax.nn.relu(dots), wib) * idx_scale
        admissible = k_pos[None, :] <= q_pos[:, None]
        score = jnp.where(admissible[None], score, -jnp.inf)
        _, sel = lax.top_k(score, topk)
        k_sel = take_rows(k, sel)
        v_sel = take_rows(v, sel)
        logits = jnp.einsum('bthd,btkhd->bthk', qb, k_sel).astype(F32) * attn_scale
        valid = sel <= q_pos[None, :, None]
        logits = jnp.where(valid[:, :, None, :], logits, -jnp.inf)
        p = jax.nn.softmax(logits, axis=-1).astype(v.dtype)
        return jnp.einsum('bthk,btkhd->bthd', p, v_sel)

    out = lax.map(block, jnp.arange(seq // Q_BLOCK))
    out = out.transpose(1, 0, 2, 3, 4).reshape(bsz, seq, hd)
    return out @ w_out


def mixer_hgrn2(x, w_in, lb_logits, norm_g, w_out, layer):
    bsz, seq, _ = x.shape
    nc = seq // C_CHUNK
    proj = x @ w_in
    q = proj[..., :C_FDIM].astype(F32)
    f_pre = proj[..., C_FDIM:2 * C_FDIM].astype(F32)
    i_in = proj[..., 2 * C_FDIM:2 * C_FDIM + D_MODEL].astype(F32)
    gate = proj[..., 2 * C_FDIM + D_MODEL:].astype(F32)
    lb_all = jnp.cumsum(jax.nn.softmax(lb_logits.astype(F32), axis=0), axis=0)
    lb = lb_all[layer] - lb_all[0]
    f = lb + (1.0 - lb) * jax.nn.sigmoid(f_pre)
    log_f = jnp.log(f)
    key = 1.0 - f

    def heads(t, d):
        return t.reshape(bsz, nc, C_CHUNK, C_HEADS, d).transpose(1, 0, 3, 2, 4)

    xs = (heads(q, C_EXPAND), heads(key, C_EXPAND), heads(log_f, C_EXPAND), heads(i_in, C_HEAD_V))
    tri = jnp.tril(jnp.ones((C_CHUNK, C_CHUNK), dtype=bool))[:, :, None]

    def step(state, inp):
        qb, kb, gb, vb = inp
        g_cum = jnp.cumsum(gb, axis=2)
        diff = g_cum[:, :, :, None, :] - g_cum[:, :, None, :, :]
        decay = jnp.where(tri, jnp.exp(jnp.where(tri, diff, 0.0)), 0.0)
        scores = jnp.einsum('bhtk,bhtsk,bhsk->bhts', qb, decay, kb)
        out = jnp.einsum('bhts,bhsv->bhtv', scores, vb) + jnp.einsum('bhtk,bhkv->bhtv', qb * jnp.exp(g_cum), state)
        g_last = g_cum[:, :, -1]
        k_dec = kb * jnp.exp(g_last[:, :, None, :] - g_cum)
        state = jnp.exp(g_last)[..., None] * state + jnp.einsum('bhsk,bhsv->bhkv', k_dec, vb)
        return state, out

    s0 = jnp.zeros((bsz, C_HEADS, C_EXPAND, C_HEAD_V), F32)
    _, o = lax.scan(step, s0, xs)
    o = o.transpose(1, 0, 3, 2, 4).reshape(bsz, seq, C_HEADS, C_HEAD_V)
    o = rms_norm(o, norm_g) * jax.nn.silu(gate.reshape(bsz, seq, C_HEADS, C_HEAD_V))
    return o.reshape(bsz, seq, D_MODEL).astype(x.dtype) @ w_out


def mixer_mla(x, positions, w_in, q_norm_g, w_uq, kv_norm_g, w_ukv, w_out):
    bsz, seq, _ = x.shape
    proj = x @ w_in
    c_q = rms_norm(proj[..., :D_Q_RANK], q_norm_g)
    c_kv = rms_norm(proj[..., D_Q_RANK:D_Q_RANK + D_KV_RANK], kv_norm_g)
    k_rope = rope(proj[..., D_Q_RANK + D_KV_RANK:][:, :, None, :], positions)[:, :, 0]
    q = (c_q @ w_uq).reshape(bsz, seq, D_HEADS, D_NOPE + D_ROPE)
    q_nope, q_rope = q[..., :D_NOPE], rope(q[..., D_NOPE:], positions)
    kv = (c_kv @ w_ukv).reshape(bsz, seq, D_HEADS, D_NOPE + D_VDIM)
    k_nope, v = kv[..., :D_NOPE], kv[..., D_NOPE:]
    scale = (D_NOPE + D_ROPE) ** -0.5
    k_pos = jnp.arange(seq)

    def block(i):
        start = i * Q_BLOCK
        q_pos = start + jnp.arange(Q_BLOCK)
        qn = lax.dynamic_slice_in_dim(q_nope, start, Q_BLOCK, axis=1)
        qr = lax.dynamic_slice_in_dim(q_rope, start, Q_BLOCK, axis=1)
        s = jnp.einsum('bthd,bshd->bhts', qn, k_nope) + jnp.einsum('bthd,bsd->bhts', qr, k_rope)
        s = jnp.where((k_pos[None, :] <= q_pos[:, None])[None, None], s.astype(F32) * scale, -jnp.inf)
        p = jax.nn.softmax(s, axis=-1).astype(v.dtype)
        return jnp.einsum('bhts,bshd->bthd', p, v)

    out = lax.map(block, jnp.arange(seq // Q_BLOCK))
    out = out.transpose(1, 0, 2, 3, 4).reshape(bsz, seq, D_HEADS * D_VDIM)
    return out @ w_out


def swiglu(x, w_gu, w_down):
    gu = x @ w_gu
    return (jax.nn.silu(gu[..., :FFN_DIM]) * gu[..., FFN_DIM:]) @ w_down


def moe_swiglu(x, w_router, w_gu, w_down):
    bsz, seq, d = x.shape
    n_tok = bsz * seq
    n_assign = n_tok * TOP_K
    xf = x.reshape(n_tok, d)
    logits = (xf @ w_router).astype(F32)
    top_logit, top_e = lax.top_k(logits, TOP_K)
    gate = jax.nn.softmax(top_logit, axis=-1)
    flat_e = top_e.reshape(-1)
    flat_tok = jnp.repeat(jnp.arange(n_tok, dtype=jnp.int32), TOP_K)
    order = jnp.argsort(flat_e)
    e_sorted, tok_sorted, gate_sorted = flat_e[order], flat_tok[order], gate.reshape(-1)[order]
    counts = jnp.zeros((N_EXPERTS,), jnp.int32).at[flat_e].add(1)
    starts = jnp.cumsum(counts) - counts
    padded = (counts + MOE_BLOCK - 1) // MOE_BLOCK * MOE_BLOCK
    padded_ends = jnp.cumsum(padded)
    padded_starts = padded_ends - padded
    dest = padded_starts[e_sorted] + jnp.arange(n_assign, dtype=jnp.int32) - starts[e_sorted]
    n_rows = (-(-n_assign // MOE_BLOCK) + N_EXPERTS) * MOE_BLOCK
    n_blocks = n_rows // MOE_BLOCK
    row_tok = jnp.full((n_rows,), n_tok, jnp.int32).at[dest].set(tok_sorted)
    row_gate = jnp.zeros((n_rows,), F32).at[dest].set(gate_sorted)
    block_e = jnp.minimum(jnp.searchsorted(padded_ends, jnp.arange(n_blocks, dtype=jnp.int32) * MOE_BLOCK, side='right'), N_EXPERTS - 1)
    x_pad = jnp.concatenate([xf, jnp.zeros((1, d), xf.dtype)], axis=0)
    x_rows = x_pad[row_tok].reshape(n_blocks, MOE_BLOCK, d)

    def expert_block(args):
        xb, e = args
        return swiglu(xb, w_gu[e], w_down[e])

    y = lax.map(expert_block, (x_rows, block_e)).reshape(n_rows, d)
    y = y * row_gate[:, None].astype(y.dtype)
    out = jnp.zeros((n_tok + 1, d), y.dtype).at[row_tok].add(y)[:n_tok]
    return out.reshape(bsz, seq, d)


def setup_inputs(seed: int = 0) -> dict:
    key = jax.random.key(seed)
    keys = iter(jax.random.split(key, 48))

    def w(shape, fan_in, scale=1.0):
        return jax.random.normal(next(keys), shape, F32) * (scale * fan_in ** -0.5)

    def gain(shape):
        return 1.0 + 0.05 * jax.random.normal(next(keys), shape, F32)

    def bias(shape):
        return 0.02 * jax.random.normal(next(keys), shape, F32)

    beta = DEEPNORM_BETA
    x = jax.random.normal(next(keys), (BATCH, SEQ, D_MODEL), F32)
    offsets = jax.random.randint(next(keys), (BATCH, 1), 0, 1024, dtype=jnp.int32)
    positions = offsets + jnp.arange(SEQ, dtype=jnp.int32)[None, :]
    return {
        "x": x,
        "positions": positions,
        "a_w_in": w((D_MODEL, 2 * A_HALF), D_MODEL),
        "a_ln_g": gain((A_HALF,)),
        "a_ln_b": bias((A_HALF,)),
        "a_w_s": w((A_GROUPS, A_CHUNK, A_CHUNK), A_CHUNK, 0.5),
        "a_b_s": gain((A_GROUPS, A_CHUNK)),
        "a_w_out": w((A_HALF, D_MODEL), A_HALF, beta),
        "b_w_in": w((D_MODEL, B_IN), D_MODEL),
        "b_w_out": w((B_HEADS * B_HEAD_DIM, D_MODEL), B_HEADS * B_HEAD_DIM, beta),
        "c_w_in": w((D_MODEL, 2 * C_FDIM + 2 * D_MODEL), D_MODEL),
        "c_lb_logits": 0.1 * jax.random.normal(next(keys), (DEPTH, C_FDIM), F32),
        "c_norm_g": gain((C_HEAD_V,)),
        "c_w_out": w((D_MODEL, D_MODEL), D_MODEL, beta),
        "d_w_in": w((D_MODEL, D_Q_RANK + D_KV_RANK + D_ROPE), D_MODEL),
        "d_q_norm_g": gain((D_Q_RANK,)),
        "d_w_uq": w((D_Q_RANK, D_HEADS * (D_NOPE + D_ROPE)), D_Q_RANK),
        "d_kv_norm_g": gain((D_KV_RANK,)),
        "d_w_ukv": w((D_KV_RANK, D_HEADS * (D_NOPE + D_VDIM)), D_KV_RANK),
        "d_w_out": w((D_HEADS * D_VDIM, D_MODEL), D_HEADS * D_VDIM, beta),
        "ffn0_w_gu": w((D_MODEL, 2 * FFN_DIM), D_MODEL),
        "ffn0_w_down": w((FFN_DIM, D_MODEL), FFN_DIM, beta),
        "moe1_w_router": w((D_MODEL, N_EXPERTS), D_MODEL),
        "moe1_w_gu": w((N_EXPERTS, D_MODEL, 2 * FFN_DIM), D_MODEL),
        "moe1_w_down": w((N_EXPERTS, FFN_DIM, D_MODEL), FFN_DIM, beta),
        "ffn2_w_gu": w((D_MODEL, 2 * FFN_DIM), D_MODEL),
        "ffn2_w_down": w((FFN_DIM, D_MODEL), FFN_DIM, beta),
        "moe3_w_router": w((D_MODEL, N_EXPERTS), D_MODEL),
        "moe3_w_gu": w((N_EXPERTS, D_MODEL, 2 * FFN_DIM), D_MODEL),
        "moe3_w_down": w((N_EXPERTS, FFN_DIM, D_MODEL), FFN_DIM, beta),
        "ln_mix_g": gain((DEPTH, D_MODEL)),
        "ln_mix_b": bias((DEPTH, D_MODEL)),
        "ln_ffn_g": gain((DEPTH, D_MODEL)),
        "ln_ffn_b": bias((DEPTH, D_MODEL)),
    }


def reference(x, positions, a_w_in, a_ln_g, a_ln_b, a_w_s, a_b_s, a_w_out,
              b_w_in, b_w_out, c_w_in, c_lb_logits, c_norm_g, c_w_out,
              d_w_in, d_q_norm_g, d_w_uq, d_kv_norm_g, d_w_ukv, d_w_out,
              ffn0_w_gu, ffn0_w_down, moe1_w_router, moe1_w_gu, moe1_w_down,
              ffn2_w_gu, ffn2_w_down, moe3_w_router, moe3_w_gu, moe3_w_down,
              ln_mix_g, ln_mix_b, ln_ffn_g, ln_ffn_b):
    dense_ffn = {0: (ffn0_w_gu, ffn0_w_down), 2: (ffn2_w_gu, ffn2_w_down)}
    moe_ffn = {1: (moe1_w_router, moe1_w_gu, moe1_w_down), 3: (moe3_w_router, moe3_w_gu, moe3_w_down)}
    for i in range(DEPTH):
        kind = i % N_MIXERS
        if kind == 0:
            h = mixer_gmlp(x, a_w_in, a_ln_g, a_ln_b, a_w_s, a_b_s, a_w_out)
        elif kind == 1:
            h = mixer_dsa(x, b_w_in, b_w_out)
        elif kind == 2:
            h = mixer_hgrn2(x, c_w_in, c_lb_logits, c_norm_g, c_w_out, i)
        else:
            h = mixer_mla(x, positions, d_w_in, d_q_norm_g, d_w_uq, d_kv_norm_g, d_w_ukv, d_w_out)
        x = layer_norm(DEEPNORM_ALPHA * x + h, ln_mix_g[i], ln_mix_b[i])
        if i % 2 == 0:
            h = swiglu(x, *dense_ffn[i])
        else:
            h = moe_swiglu(x, *moe_ffn[i])
        x = layer_norm(DEEPNORM_ALPHA * x + h, ln_ffn_g[i], ln_ffn_b[i])
    return x
```

```python
import functools

import jax
import jax.numpy as jnp
from jax import lax
from jax.experimental import pallas as pl
from jax.experimental.pallas import tpu as pltpu

F32 = jnp.float32
BF16 = jnp.bfloat16
I32 = jnp.int32

D_MODEL = 1024
DEPTH = 4
ALPHA = (2.0 * DEPTH) ** 0.25
LN_EPS = 1e-5
RMS_EPS = 1e-6

A_CHUNK = 128
A_HALF = 2 * D_MODEL
A_GROUPS = 8
A_GROUP_DIM = A_HALF // A_GROUPS

B_HEADS = 16
B_HEAD_DIM = 64
B_IDX_HEADS = 4
B_IDX_DIM = 64
B_TOPK_MAX = 256

C_HEADS = 8
C_EXPAND = 128
C_HEAD_V = 128
C_FDIM = C_HEADS * C_EXPAND
C_SUB = 16
C_HG = 4
C_TS = 512

D_HEADS = 16
D_NOPE = 64
D_ROPE = 32
D_VDIM = 64
D_Q_RANK = 256
D_KV_RANK = 128
ROPE_BASE = 10000.0

FFN_DIM = 3584
N_EXPERTS = 8
TOP_K = 2

LANES = 128
VMEM_LIMIT = 52 * 1024 * 1024
NEG = -1e30
INT_MIN = -2147483648
HIGHEST = lax.Precision.HIGHEST


def _params(*sem):
    return pltpu.CompilerParams(dimension_semantics=sem, vmem_limit_bytes=VMEM_LIMIT)


def _layer_norm(y, g, b):
    mu = jnp.mean(y, axis=-1, keepdims=True)
    d = y - mu
    var = jnp.mean(d * d, axis=-1, keepdims=True)
    return d * lax.rsqrt(var + LN_EPS) * g + b


def _nt_dot(a, b, precision=None):
    return lax.dot_general(a, b, (((1,), (1,)), ((), ())), preferred_element_type=F32, precision=precision)


def _mm_body(a_ref, w_ref, o_ref, *, act, precision):
    a = a_ref[...]
    if precision is None:
        a = a.astype(BF16)
    acc = jnp.dot(a, w_ref[...], preferred_element_type=F32, precision=precision)
    if act == "gelu":
        acc = jax.nn.gelu(acc)
    o_ref[...] = acc.astype(o_ref.dtype)


def _matmul(a, w, *, tm, tn, out_dtype, act=None, precision=None):
    m, k = a.shape
    n = w.shape[1]
    return pl.pallas_call(
        functools.partial(_mm_body, act=act, precision=precision),
        grid=(m // tm, n // tn),
        in_specs=[pl.BlockSpec((tm, k), lambda i, j: (i, 0)), pl.BlockSpec((k, tn), lambda i, j: (0, j))],
        out_specs=pl.BlockSpec((tm, tn), lambda i, j: (i, j)),
        out_shape=jax.ShapeDtypeStruct((m, n), out_dtype),
        compiler_params=_params("parallel", "parallel"),
    )(a, w)


def _proj_ln_body(a_ref, w_ref, x_ref, g_ref, b_ref, o_ref):
    h = jnp.dot(a_ref[...].astype(BF16), w_ref[...], preferred_element_type=F32)
    o_ref[...] = _layer_norm(ALPHA * x_ref[...] + h, g_ref[...], b_ref[...])


def _proj_ln(a, w, x, g, b, *, tm=512):
    m, k = a.shape
    return pl.pallas_call(
        _proj_ln_body,
        grid=(m // tm,),
        in_specs=[pl.BlockSpec((tm, k), lambda i: (i, 0)), pl.BlockSpec((k, D_MODEL), lambda i: (0, 0)),
                  pl.BlockSpec((tm, D_MODEL), lambda i: (i, 0)), pl.BlockSpec((1, D_MODEL), lambda i: (0, 0)),
                  pl.BlockSpec((1, D_MODEL), lambda i: (0, 0))],
        out_specs=pl.BlockSpec((tm, D_MODEL), lambda i: (i, 0)),
        out_shape=jax.ShapeDtypeStruct((m, D_MODEL), F32),
        compiler_params=_params("parallel"),
    )(a, w, x, g, b)


def _ffn_body(x_ref, wg_ref, wu_ref, wd_ref, g_ref, b_ref, o_ref, acc_ref, xb_ref):
    j = pl.program_id(1)

    @pl.when(j == 0)
    def _():
        acc_ref[...] = jnp.zeros_like(acc_ref)
        xb_ref[...] = x_ref[...].astype(BF16)

    xb = xb_ref[...]
    gate = jnp.dot(xb, wg_ref[...], preferred_element_type=F32)
    up = jnp.dot(xb, wu_ref[...], preferred_element_type=F32)
    h = (gate * jax.nn.sigmoid(gate)) * up
    acc_ref[...] += jnp.dot(h.astype(BF16), wd_ref[...], preferred_element_type=F32)

    @pl.when(j == pl.num_programs(1) - 1)
    def _():
        o_ref[...] = _layer_norm(ALPHA * x_ref[...] + acc_ref[...], g_ref[...], b_ref[...])


def _dense_ffn(x, w_gu, w_down, g, b, *, tm=512, tf=512):
    m = x.shape[0]
    nf = FFN_DIM // tf
    return pl.pallas_call(
        _ffn_body,
        grid=(m // tm, nf),
        in_specs=[pl.BlockSpec((tm, D_MODEL), lambda i, j: (i, 0)),
                  pl.BlockSpec((D_MODEL, tf), lambda i, j: (0, j)),
                  pl.BlockSpec((D_MODEL, tf), lambda i, j: (0, nf + j)),
                  pl.BlockSpec((tf, D_MODEL), lambda i, j: (j, 0)),
                  pl.BlockSpec((1, D_MODEL), lambda i, j: (0, 0)),
                  pl.BlockSpec((1, D_MODEL), lambda i, j: (0, 0))],
        out_specs=pl.BlockSpec((tm, D_MODEL), lambda i, j: (i, 0)),
        out_shape=jax.ShapeDtypeStruct((m, D_MODEL), F32),
        scratch_shapes=[pltpu.VMEM((tm, D_MODEL), F32), pltpu.VMEM((tm, D_MODEL), BF16)],
        compiler_params=_params("parallel", "arbitrary"),
    )(x, w_gu, w_gu, w_down, g, b)


def _router_body(x_ref, wr_ref, e_ref, g_ref):
    logits = _nt_dot(wr_ref[...], x_ref[...], precision=HIGHEST)
    eid = lax.broadcasted_iota(I32, logits.shape, 0).astype(F32)
    m1 = jnp.max(logits, axis=0, keepdims=True)
    i1 = jnp.min(jnp.where(logits == m1, eid, float(N_EXPERTS)), axis=0, keepdims=True)
    rest = jnp.where(eid == i1, -jnp.inf, logits)
    m2 = jnp.max(rest, axis=0, keepdims=True)
    i2 = jnp.min(jnp.where(rest == m2, eid, float(N_EXPERTS)), axis=0, keepdims=True)
    e2 = jnp.exp(m2 - m1)
    den = 1.0 + e2
    e_ref[...] = jnp.concatenate([i1, i2], axis=0).astype(I32)
    g_ref[...] = jnp.concatenate([1.0 / den, e2 / den], axis=0)


def _router(x, w_router_t, *, tm=512):
    m = x.shape[0]
    return pl.pallas_call(
        _router_body,
        grid=(m // tm,),
        in_specs=[pl.BlockSpec((tm, D_MODEL), lambda i: (i, 0)), pl.BlockSpec((N_EXPERTS, D_MODEL), lambda i: (0, 0))],
        out_specs=[pl.BlockSpec((TOP_K, tm), lambda i: (0, i)), pl.BlockSpec((TOP_K, tm), lambda i: (0, i))],
        out_shape=[jax.ShapeDtypeStruct((TOP_K, m), I32), jax.ShapeDtypeStruct((TOP_K, m), F32)],
        compiler_params=_params("parallel"),
    )(x, w_router_t)


def _row_gather(tok_ref, src_hbm, dst_ref, sem, n_rows):
    def start(r, c):
        pltpu.make_async_copy(src_hbm.at[pl.ds(tok_ref[0, 0, r], 1)], dst_ref.at[pl.ds(r, 1)], sem).start()
        return c

    lax.fori_loop(0, n_rows, start, 0)

    def wait(r, c):
        pltpu.make_async_copy(src_hbm.at[pl.ds(0, 1)], dst_ref.at[pl.ds(r, 1)], sem).wait()
        return c

    lax.fori_loop(0, n_rows, wait, 0)


def _moe_body(te_ref, tok_ref, x_hbm, wg_ref, wu_ref, wd_ref, y_ref, xrow_ref, xb_ref, acc_ref, sem):
    del te_ref
    j = pl.program_id(1)

    @pl.when(j == 0)
    def _():
        _row_gather(tok_ref, x_hbm, xrow_ref, sem, xrow_ref.shape[0])
        xb_ref[...] = xrow_ref[...].astype(BF16)
        acc_ref[...] = jnp.zeros_like(acc_ref)

    xb = xb_ref[...]
    gate = jnp.dot(xb, wg_ref[...], preferred_element_type=F32)
    up = jnp.dot(xb, wu_ref[...], preferred_element_type=F32)
    h = (gate * jax.nn.sigmoid(gate)) * up
    acc_ref[...] += jnp.dot(h.astype(BF16), wd_ref[...], preferred_element_type=F32)

    @pl.when(j == pl.num_programs(1) - 1)
    def _():
        y_ref[...] = acc_ref[...]


def _combine_body(pos_ref, x_ref, gt_ref, y_hbm, g_ref, b_ref, o_ref, rows_ref, sem):
    tq = x_ref.shape[0]
    _row_gather(pos_ref, y_hbm, rows_ref, sem, 2 * tq)
    gt = gt_ref[...]
    h = gt[:, 0:1] * rows_ref[0:tq, :] + gt[:, 1:2] * rows_ref[tq:2 * tq, :]
    o_ref[...] = _layer_norm(ALPHA * x_ref[...] + h, g_ref[...], b_ref[...])


def _moe_ffn(x, w_router_t, w_gu, w_down, g, b, *, tm=512, tf=512, tq=256):
    m = x.shape[0]
    n_assign = m * TOP_K
    n_tiles = n_assign // tm + N_EXPERTS
    n_rows = n_tiles * tm
    nf = FFN_DIM // tf

    eidx, gate = _router(x, w_router_t)
    flat_e = eidx.reshape(-1)
    onehot = (flat_e[:, None] == jnp.arange(N_EXPERTS, dtype=I32)[None, :]).astype(I32)
    csum = jnp.cumsum(onehot, axis=0)
    rank = jnp.sum(onehot * csum, axis=1) - 1
    counts = csum[-1]
    padded = (counts + tm - 1) // tm * tm
    padded_ends = jnp.cumsum(padded)
    padded_starts = padded_ends - padded
    dest = padded_starts[flat_e] + rank
    tok = jnp.tile(jnp.arange(m, dtype=I32), TOP_K)
    row_tok = jnp.zeros((n_rows,), I32).at[dest].set(tok)
    tile_e = jnp.minimum(jnp.searchsorted(padded_ends, jnp.arange(n_tiles, dtype=I32) * tm, side="right"),
                         N_EXPERTS - 1).astype(I32)

    y = pl.pallas_call(
        _moe_body,
        grid_spec=pltpu.PrefetchScalarGridSpec(
            num_scalar_prefetch=1,
            grid=(n_tiles, nf),
            in_specs=[pl.BlockSpec((1, 1, tm), lambda i, j, te: (i, 0, 0), memory_space=pltpu.SMEM),
                      pl.BlockSpec(memory_space=pl.ANY),
                      pl.BlockSpec((None, D_MODEL, tf), lambda i, j, te: (te[i], 0, j)),
                      pl.BlockSpec((None, D_MODEL, tf), lambda i, j, te: (te[i], 0, nf + j)),
                      pl.BlockSpec((None, tf, D_MODEL), lambda i, j, te: (te[i], j, 0))],
            out_specs=pl.BlockSpec((tm, D_MODEL), lambda i, j, te: (i, 0)),
            scratch_shapes=[pltpu.VMEM((tm, D_MODEL), F32), pltpu.VMEM((tm, D_MODEL), BF16),
                            pltpu.VMEM((tm, D_MODEL), F32), pltpu.SemaphoreType.DMA(())]),
        out_shape=jax.ShapeDtypeStruct((n_rows, D_MODEL), F32),
        compiler_params=_params("arbitrary", "arbitrary"),
    )(tile_e, row_tok.reshape(n_tiles, 1, tm), x, w_gu, w_gu, w_down)

    pos = jnp.concatenate([dest[:m].reshape(m // tq, 1, tq), dest[m:].reshape(m // tq, 1, tq)], axis=2)
    return pl.pallas_call(
        _combine_body,
        grid=(m // tq,),
        in_specs=[pl.BlockSpec((1, 1, 2 * tq), lambda i: (i, 0, 0), memory_space=pltpu.SMEM),
                  pl.BlockSpec((tq, D_MODEL), lambda i: (i, 0)),
                  pl.BlockSpec((tq, TOP_K), lambda i: (i, 0)),
                  pl.BlockSpec(memory_space=pl.ANY),
                  pl.BlockSpec((1, D_MODEL), lambda i: (0, 0)),
                  pl.BlockSpec((1, D_MODEL), lambda i: (0, 0))],
        out_specs=pl.BlockSpec((tq, D_MODEL), lambda i: (i, 0)),
        out_shape=jax.ShapeDtypeStruct((m, D_MODEL), F32),
        scratch_shapes=[pltpu.VMEM((2 * tq, D_MODEL), F32), pltpu.SemaphoreType.DMA(())],
        compiler_params=_params("arbitrary"),
    )(pos, x, gate.T, y, g, b)


def _gmlp_gate_body(u_ref, v_ref, lg_ref, lb_ref, ws_ref, bs_ref, o_ref):
    tm = u_ref.shape[0]
    v = _layer_norm(v_ref[...], lg_ref[...], lb_ref[...]).astype(BF16)
    r = lax.broadcasted_iota(I32, (A_CHUNK, A_CHUNK), 0)
    c = lax.broadcasted_iota(I32, (A_CHUNK, A_CHUNK), 1)
    causal = c <= r
    bs = bs_ref[...]
    for grp in range(A_GROUPS):
        wc = jnp.where(causal, ws_ref[grp], 0.0).astype(BF16)
        lo = grp * A_GROUP_DIM
        for ch in range(tm // A_CHUNK):
            r0 = ch * A_CHUNK
            s = jnp.dot(wc, v[r0:r0 + A_CHUNK, lo:lo + A_GROUP_DIM], preferred_element_type=F32)
            s = s + bs[:, grp:grp + 1]
            o_ref[r0:r0 + A_CHUNK, lo:lo + A_GROUP_DIM] = (u_ref[r0:r0 + A_CHUNK, lo:lo + A_GROUP_DIM] * s).astype(BF16)


def _gmlp_mixer(x, w_in, ln_g, ln_b, w_s, b_s_t, w_out, g, b, *, tm=256):
    m = x.shape[0]
    uv = _matmul(x, w_in, tm=512, tn=1024, out_dtype=F32, act="gelu")
    gated = pl.pallas_call(
        _gmlp_gate_body,
        grid=(m // tm,),
        in_specs=[pl.BlockSpec((tm, A_HALF), lambda i: (i, 0)), pl.BlockSpec((tm, A_HALF), lambda i: (i, 1)),
                  pl.BlockSpec((1, A_HALF), lambda i: (0, 0)), pl.BlockSpec((1, A_HALF), lambda i: (0, 0)),
                  pl.BlockSpec((A_GROUPS, A_CHUNK, A_CHUNK), lambda i: (0, 0, 0)),
                  pl.BlockSpec((A_CHUNK, A_GROUPS), lambda i: (0, 0))],
        out_specs=pl.BlockSpec((tm, A_HALF), lambda i: (i, 0)),
        out_shape=jax.ShapeDtypeStruct((m, A_HALF), BF16),
        compiler_params=_params("parallel"),
    )(uv, uv, ln_g, ln_b, w_s, b_s_t)
    return _proj_ln(gated, w_out, x, g, b)


def _flash(q, k_ref, v_ref, k_lane0, v_lane0, n_kc, ck, scale, bias_fn):
    tq, dk = q.shape

    def body(c, carry):
        m, l, acc = carry
        c0 = pl.multiple_of(c * ck, ck)
        kc = k_ref[pl.ds(c0, ck), k_lane0:k_lane0 + dk]
        s = _nt_dot(q, kc) * scale + bias_fn(c0)
        m_new = jnp.maximum(m, jnp.max(s, axis=1, keepdims=True))
        a = jnp.exp(m - m_new)
        p = jnp.exp(s - m_new)
        l = a * l + jnp.sum(p, axis=1, keepdims=True)
        vc = v_ref[pl.ds(c0, ck), v_lane0:v_lane0 + LANES]
        acc = a * acc + jnp.dot(p.astype(BF16), vc, preferred_element_type=F32)
        return m_new, l, acc

    init = (jnp.full((tq, 1), NEG, F32), jnp.zeros((tq, 1), F32), jnp.zeros((tq, LANES), F32))
    _, l, acc = lax.fori_loop(0, n_kc, body, init)
    return acc / l


def _dsa_body(q_ref, k_ref, v_ref, qi_ref, ki_ref, wi_ref, o_ref, key_ref, bias_ref, *, tq, ck, topk):
    i = pl.program_id(1)
    n_c = i + 1
    n_kc = (i * tq + tq + ck - 1) // ck
    row = i * tq + lax.broadcasted_iota(I32, (tq, LANES), 0)
    lane = lax.broadcasted_iota(I32, (tq, LANES), 1)
    idx_scale = (B_IDX_DIM * B_IDX_HEADS) ** -0.5

    wi = wi_ref[...]
    qh = []
    for h in range(B_IDX_HEADS):
        q2 = qi_ref[:, (h // 2) * LANES:(h // 2 + 1) * LANES]
        in_head = (lane >= B_IDX_DIM) if h % 2 else (lane < B_IDX_DIM)
        qh.append(jnp.where(in_head, q2, 0.0))

    def score_chunk(c, carry):
        c0 = pl.multiple_of(c * LANES, LANES)

        @pl.when(c <= i)
        def _():
            kc = ki_ref[pl.ds(c0, LANES), :]
            score = jnp.zeros((tq, LANES), F32)
            for h in range(B_IDX_HEADS):
                dots = _nt_dot(qh[h], kc, precision=HIGHEST)
                score = score + jnp.maximum(dots, 0.0) * wi[:, h:h + 1]
            score = score * idx_scale
            score = jnp.where(score == 0.0, 0.0, score)
            bits = lax.bitcast_convert_type(score, I32)
            key = bits ^ (lax.shift_right_arithmetic(bits, 31) & jnp.int32(0x7FFFFFFF))
            key_ref[:, pl.ds(c0, LANES)] = jnp.where(c0 + lane <= row, key, INT_MIN)

        @pl.when(c > i)
        def _():
            key_ref[:, pl.ds(c0, LANES)] = jnp.full((tq, LANES), INT_MIN, I32)

        return carry

    lax.fori_loop(0, n_kc * (ck // LANES), score_chunk, 0)

    def count(pred):
        def cb(c, acc):
            c0 = pl.multiple_of(c * LANES, LANES)
            return acc + jnp.where(pred(key_ref[:, pl.ds(c0, LANES)], c0 + lane), 1.0, 0.0)

        acc = lax.fori_loop(0, n_c, cb, jnp.zeros((tq, LANES), F32))
        return jnp.sum(acc, axis=1, keepdims=True)

    def bit_step(bi, ans):
        cand = ans + lax.shift_left(jnp.int32(1), 31 - bi)
        return jnp.where(count(lambda kk, col: kk >= cand) >= float(topk), cand, ans)

    thr = lax.fori_loop(0, 32, bit_step, jnp.full((tq, 1), INT_MIN, I32))

    need = float(topk) - count(lambda kk, col: kk > thr)

    nbits = (key_ref.shape[1] - 1).bit_length()

    def idx_step(bi, ans):
        cand = ans + lax.shift_left(jnp.int32(1), nbits - 1 - bi)
        return jnp.where(count(lambda kk, col: (kk == thr) & (col < cand)) < need, cand, ans)

    jcut = lax.fori_loop(0, nbits, idx_step, jnp.zeros((tq, 1), I32))
    jcut = jnp.where(thr == INT_MIN, -1, jcut)

    def bias_chunk(c, carry):
        c0 = pl.multiple_of(c * LANES, LANES)
        kk = key_ref[:, pl.ds(c0, LANES)]
        sel = (kk > thr) | ((kk == thr) & (c0 + lane <= jcut))
        bias_ref[:, pl.ds(c0, LANES)] = jnp.where(sel, 0.0, NEG)
        return carry

    lax.fori_loop(0, n_kc * (ck // LANES), bias_chunk, 0)

    scale = B_HEAD_DIM ** -0.5
    bias_fn = lambda c0: bias_ref[:, pl.ds(c0, ck)]
    for p in range(B_HEADS // 2):
        q2 = q_ref[:, p * LANES:(p + 1) * LANES].astype(F32)
        q_even = jnp.where(lane < B_HEAD_DIM, q2, 0.0).astype(BF16)
        q_odd = jnp.where(lane >= B_HEAD_DIM, q2, 0.0).astype(BF16)
        o_even = _flash(q_even, k_ref, v_ref, p * LANES, p * LANES, n_kc, ck, scale, bias_fn)
        o_odd = _flash(q_odd, k_ref, v_ref, p * LANES, p * LANES, n_kc, ck, scale, bias_fn)
        o_ref[:, p * LANES:(p + 1) * LANES] = jnp.where(lane < B_HEAD_DIM, o_even, o_odd).astype(BF16)


def _dsa_mixer(x, w_qkv, w_idx, w_out, g, b, *, batch, seq, tq=128, ck=512):
    m = x.shape[0]
    nq = seq // tq
    hd = B_HEADS * B_HEAD_DIM
    topk = min(B_TOPK_MAX, seq // 4)
    qkv = _matmul(x, w_qkv, tm=512, tn=1024, out_dtype=BF16)
    idx = _matmul(x, w_idx, tm=512, tn=512, out_dtype=F32, precision=HIGHEST)
    attn = pl.pallas_call(
        functools.partial(_dsa_body, tq=tq, ck=ck, topk=topk),
        grid=(batch, nq),
        in_specs=[pl.BlockSpec((tq, hd), lambda bi, i: (bi * nq + i, 0)),
                  pl.BlockSpec((seq, hd), lambda bi, i: (bi, 1)),
                  pl.BlockSpec((seq, hd), lambda bi, i: (bi, 2)),
                  pl.BlockSpec((tq, 2 * LANES), lambda bi, i: (bi * nq + i, 0)),
                  pl.BlockSpec((seq, LANES), lambda bi, i: (bi, 2)),
                  pl.BlockSpec((tq, LANES), lambda bi, i: (bi * nq + i, 3))],
        out_specs=pl.BlockSpec((tq, hd), lambda bi, i: (bi * nq + i, 0)),
        out_shape=jax.ShapeDtypeStruct((m, hd), BF16),
        scratch_shapes=[pltpu.VMEM((tq, seq), I32), pltpu.VMEM((tq, seq), F32)],
        compiler_params=_params("parallel", "arbitrary"),
    )(qkv, qkv, qkv, idx, idx, idx)
    return _proj_ln(attn, w_out, x, g, b)


def _hgrn2_body(q_ref, f_ref, i_ref, gt_ref, lbl_ref, ng_ref, o_ref, st_ref, *, layer):
    @pl.when(pl.program_id(2) == 0)
    def _():
        st_ref[...] = jnp.zeros_like(st_ref)

    lg = lbl_ref[...]
    e = jnp.exp(lg - jnp.max(lg, axis=0, keepdims=True))
    sm = e / jnp.sum(e, axis=0, keepdims=True)
    lb_all = [sm[0:1]]
    for d in range(1, DEPTH):
        lb_all.append(lb_all[-1] + sm[d:d + 1])
    lb = lb_all[layer] - lb_all[0]
    ng = ng_ref[...]
    rid = lax.broadcasted_iota(I32, (C_SUB, LANES), 0)

    def sub_chunk(j, carry):
        r0 = pl.multiple_of(j * C_SUB, C_SUB)
        for h in range(C_HG):
            ls = slice(h * LANES, (h + 1) * LANES)
            qb = q_ref[pl.ds(r0, C_SUB), ls]
            lbh = lb[:, ls]
            f = lbh + (1.0 - lbh) * jax.nn.sigmoid(f_ref[pl.ds(r0, C_SUB), ls])
            gb = jnp.log(f)
            kb = 1.0 - f
            vb = i_ref[pl.ds(r0, C_SUB), ls]
            gc = gb
            for sh in (1, 2, 4, 8):
                gc = gc + jnp.where(rid >= sh, pltpu.roll(gc, sh, 0), 0.0)
            glast = gc[C_SUB - 1:C_SUB, :]
            st = st_ref[h]
            out = _nt_dot((qb * jnp.exp(gc)).astype(BF16), st.astype(BF16))
            for s in range(C_SUB):
                keep = rid >= s
                dec = jnp.exp(jnp.where(keep, gc - gc[s:s + 1, :], 0.0))
                a = jnp.sum(jnp.where(keep, qb * dec * kb[s:s + 1, :], 0.0), axis=1, keepdims=True)
                out = out + a * vb[s:s + 1, :]
            kdec = kb * jnp.exp(glast - gc)
            kv_t = lax.dot_general(vb.astype(BF16), kdec.astype(BF16), (((0,), (0,)), ((), ())),
                                   preferred_element_type=F32)
            st_ref[h] = st * jnp.exp(glast) + kv_t
            gate = gt_ref[pl.ds(r0, C_SUB), ls]
            o = out * lax.rsqrt(jnp.mean(out * out, axis=1, keepdims=True) + RMS_EPS) * ng
            o_ref[pl.ds(r0, C_SUB), ls] = (o * (gate * jax.nn.sigmoid(gate))).astype(BF16)
        return carry

    lax.fori_loop(0, q_ref.shape[0] // C_SUB, sub_chunk, 0)


def _hgrn2_mixer(x, w_in, lb_logits, norm_g, w_out, g, b, *, batch, seq, layer):
    m = x.shape[0]
    ts = min(C_TS, seq)
    ns = seq // ts
    w = C_HG * LANES
    nhb = C_HEADS // C_HG
    proj = _matmul(x, w_in, tm=512, tn=1024, out_dtype=F32)
    spec = lambda blk: pl.BlockSpec((ts, w), lambda bi, hb, s: (bi * ns + s, blk * nhb + hb))
    o = pl.pallas_call(
        functools.partial(_hgrn2_body, layer=layer),
        grid=(batch, nhb, ns),
        in_specs=[spec(0), spec(1), spec(2), spec(3),
                  pl.BlockSpec((DEPTH, w), lambda bi, hb, s: (0, hb)),
                  pl.BlockSpec((1, LANES), lambda bi, hb, s: (0, 0))],
        out_specs=pl.BlockSpec((ts, w), lambda bi, hb, s: (bi * ns + s, hb)),
        out_shape=jax.ShapeDtypeStruct((m, D_MODEL), BF16),
        scratch_shapes=[pltpu.VMEM((C_HG, C_HEAD_V, C_EXPAND), F32)],
        compiler_params=_params("parallel", "parallel", "arbitrary"),
    )(proj, proj, proj, proj, lb_logits, norm_g)
    return _proj_ln(o, w_out, x, g, b)


def _mla_prep_body(p_ref, pos_ref, invf_ref, qg_ref, kvg_ref, wq_ref, wk_ref, wv_ref, q_ref, k_ref, v_ref):
    tm = p_ref.shape[0]
    lane = lax.broadcasted_iota(I32, (tm, LANES), 1)
    ang = pos_ref[...].astype(F32) * invf_ref[...]
    cos_t = jnp.cos(ang)
    sin_t = jnp.sin(ang)
    cq = p_ref[:, 0:D_Q_RANK]
    cq = cq * lax.rsqrt(jnp.mean(cq * cq, axis=1, keepdims=True) + RMS_EPS) * qg_ref[...]
    ckv = p_ref[:, D_Q_RANK:D_Q_RANK + D_KV_RANK]
    ckv = ckv * lax.rsqrt(jnp.mean(ckv * ckv, axis=1, keepdims=True) + RMS_EPS) * kvg_ref[...]
    k_rot = p_ref[:, 3 * LANES:4 * LANES] * cos_t + p_ref[:, 4 * LANES:5 * LANES] * sin_t
    q_tab = jnp.where(lane < D_NOPE + D_ROPE, cos_t, sin_t)
    cqb = cq.astype(BF16)
    ckvb = ckv.astype(BF16)
    for h in range(D_HEADS):
        ls = slice(h * LANES, (h + 1) * LANES)
        qh = jnp.dot(cqb, wq_ref[:, ls], preferred_element_type=F32)
        q_ref[:, ls] = (qh * q_tab).astype(BF16)
        kh = jnp.dot(ckvb, wk_ref[:, ls], preferred_element_type=F32)
        k_ref[:, ls] = (kh + k_rot).astype(BF16)
    v_ref[...] = jnp.dot(ckvb, wv_ref[...], preferred_element_type=F32).astype(BF16)


def _mla_attn_body(q_ref, k_ref, v_ref, o_ref, *, tq, ck):
    i = pl.program_id(2)
    n_kc = (i * tq + tq + ck - 1) // ck
    row = i * tq + lax.broadcasted_iota(I32, (tq, ck), 0)
    colc = lax.broadcasted_iota(I32, (tq, ck), 1)
    lane = lax.broadcasted_iota(I32, (tq, LANES), 1)
    scale = (D_NOPE + D_ROPE) ** -0.5
    bias_fn = lambda c0: jnp.where(c0 + colc <= row, 0.0, NEG)
    o_even = _flash(q_ref[:, 0:LANES], k_ref, v_ref, 0, 0, n_kc, ck, scale, bias_fn)
    o_odd = _flash(q_ref[:, LANES:2 * LANES], k_ref, v_ref, LANES, 0, n_kc, ck, scale, bias_fn)
    o_ref[...] = jnp.where(lane < D_VDIM, o_even, o_odd).astype(BF16)


def _mla_mixer(x, positions, w_in, invf, q_norm_g, kv_norm_g, w_q, w_k, w_v, w_out, g, b, *, batch, seq, tm=256, tq=256, ck=256):
    m = x.shape[0]
    nq = seq // tq
    hw = D_HEADS * LANES
    proj = _matmul(x, w_in, tm=512, tn=w_in.shape[1], out_dtype=F32)
    q, k, v = pl.pallas_call(
        _mla_prep_body,
        grid=(m // tm,),
        in_specs=[pl.BlockSpec((tm, w_in.shape[1]), lambda i: (i, 0)), pl.BlockSpec((tm, 1), lambda i: (i, 0)),
                  pl.BlockSpec((1, LANES), lambda i: (0, 0)), pl.BlockSpec((1, D_Q_RANK), lambda i: (0, 0)),
                  pl.BlockSpec((1, D_KV_RANK), lambda i: (0, 0)), pl.BlockSpec((D_Q_RANK, hw), lambda i: (0, 0)),
                  pl.BlockSpec((D_KV_RANK, hw), lambda i: (0, 0)),
                  pl.BlockSpec((D_KV_RANK, D_HEADS * D_VDIM), lambda i: (0, 0))],
        out_specs=[pl.BlockSpec((tm, hw), lambda i: (i, 0)), pl.BlockSpec((tm, hw), lambda i: (i, 0)),
                   pl.BlockSpec((tm, D_HEADS * D_VDIM), lambda i: (i, 0))],
        out_shape=[jax.ShapeDtypeStruct((m, hw), BF16), jax.ShapeDtypeStruct((m, hw), BF16),
                   jax.ShapeDtypeStruct((m, D_HEADS * D_VDIM), BF16)],
        compiler_params=_params("parallel"),
    )(proj, positions.reshape(m, 1), invf, q_norm_g, kv_norm_g, w_q, w_k, w_v)
    attn = pl.pallas_call(
        functools.partial(_mla_attn_body, tq=tq, ck=ck),
        grid=(batch, D_HEADS // 2, nq),
        in_specs=[pl.BlockSpec((tq, 2 * LANES), lambda bi, hp, i: (bi * nq + i, hp)),
                  pl.BlockSpec((seq, 2 * LANES), lambda bi, hp, i: (bi, hp)),
                  pl.BlockSpec((seq, LANES), lambda bi, hp, i: (bi, hp))],
        out_specs=pl.BlockSpec((tq, LANES), lambda bi, hp, i: (bi * nq + i, hp)),
        out_shape=jax.ShapeDtypeStruct((m, D_HEADS * D_VDIM), BF16),
        compiler_params=_params("parallel", "parallel", "arbitrary"),
    )(q, k, v)
    return _proj_ln(attn, w_out, x, g, b)


def _rotate_half_cols(w):
    half = w.shape[-1] // 2
    return jnp.concatenate([-w[..., half:], w[..., :half]], axis=-1)


def _mla_weights(d_w_in, d_w_uq, d_w_ukv):
    zeros = lambda *s: jnp.zeros(s, F32)
    w_kr = d_w_in[:, D_Q_RANK + D_KV_RANK:]
    w_krh = _rotate_half_cols(w_kr)
    z64 = zeros(D_MODEL, D_NOPE)
    w_in = jnp.concatenate([d_w_in[:, :D_Q_RANK + D_KV_RANK], z64, w_kr, w_kr, z64, w_krh, w_krh], axis=1)
    uq = d_w_uq.reshape(D_Q_RANK, D_HEADS, D_NOPE + D_ROPE)
    w_q = jnp.concatenate([uq, _rotate_half_cols(uq[..., D_NOPE:])], axis=-1).reshape(D_Q_RANK, D_HEADS * LANES)
    ukv = d_w_ukv.reshape(D_KV_RANK, D_HEADS, D_NOPE + D_VDIM)
    w_k = jnp.concatenate([ukv[..., :D_NOPE], zeros(D_KV_RANK, D_HEADS, LANES - D_NOPE)], axis=-1)
    w_k = w_k.reshape(D_KV_RANK, D_HEADS * LANES)
    w_v = ukv[..., D_NOPE:].reshape(D_KV_RANK, D_HEADS * D_VDIM)
    half = D_ROPE // 2
    inv_freq = ROPE_BASE ** (-jnp.arange(half, dtype=F32) / half)
    invf = jnp.concatenate([jnp.zeros((D_NOPE,), F32), jnp.tile(inv_freq, 4)]).reshape(1, LANES)
    return w_in.astype(BF16), w_q.astype(BF16), w_k.astype(BF16), w_v.astype(BF16), invf


def _dsa_weights(b_w_in):
    hd = B_HEADS * B_HEAD_DIM
    o = 3 * hd
    nqi = B_IDX_HEADS * B_IDX_DIM
    w_qi = b_w_in[:, o:o + nqi]
    w_ki = b_w_in[:, o + nqi:o + nqi + B_IDX_DIM]
    w_wi = b_w_in[:, o + nqi + B_IDX_DIM:]
    pad = jnp.zeros((D_MODEL, LANES - B_IDX_HEADS), F32)
    w_idx = jnp.concatenate([w_qi, w_ki, w_ki, w_wi, pad], axis=1)
    return b_w_in[:, :o].astype(BF16), w_idx


def kernel(x, positions, a_w_in, a_ln_g, a_ln_b, a_w_s, a_b_s, a_w_out, b_w_in, b_w_out, c_w_in, c_lb_logits, c_norm_g, c_w_out, d_w_in, d_q_norm_g, d_w_uq, d_kv_norm_g, d_w_ukv, d_w_out, ffn0_w_gu, ffn0_w_down, moe1_w_router, moe1_w_gu, moe1_w_down, ffn2_w_gu, ffn2_w_down, moe3_w_router, moe3_w_gu, moe3_w_down, ln_mix_g, ln_mix_b, ln_ffn_g, ln_ffn_b):
    batch, seq, d = x.shape
    m = batch * seq
    row = lambda v: v.reshape(1, -1).astype(F32)
    bf = lambda w: w.astype(BF16)
    h = x.reshape(m, d)

    h = _gmlp_mixer(h, bf(a_w_in), row(a_ln_g), row(a_ln_b), a_w_s, a_b_s.T, bf(a_w_out),
                    row(ln_mix_g[0]), row(ln_mix_b[0]))
    h = _dense_ffn(h, bf(ffn0_w_gu), bf(ffn0_w_down), row(ln_ffn_g[0]), row(ln_ffn_b[0]))

    w_qkv, w_idx = _dsa_weights(b_w_in)
    h = _dsa_mixer(h, w_qkv, w_idx, bf(b_w_out), row(ln_mix_g[1]), row(ln_mix_b[1]), batch=batch, seq=seq)
    h = _moe_ffn(h, moe1_w_router.T, bf(moe1_w_gu), bf(moe1_w_down), row(ln_ffn_g[1]), row(ln_ffn_b[1]))

    h = _hgrn2_mixer(h, bf(c_w_in), c_lb_logits, row(c_norm_g), bf(c_w_out), row(ln_mix_g[2]), row(ln_mix_b[2]),
                     batch=batch, seq=seq, layer=2)
    h = _dense_ffn(h, bf(ffn2_w_gu), bf(ffn2_w_down), row(ln_ffn_g[2]), row(ln_ffn_b[2]))

    w_in, w_q, w_k, w_v, invf = _mla_weights(d_w_in, d_w_uq, d_w_ukv)
    h = _mla_mixer(h, positions, w_in, invf, row(d_q_norm_g), row(d_kv_norm_g), w_q, w_k, w_v, bf(d_w_out),
                   row(ln_mix_g[3]), row(ln_mix_b[3]), batch=batch, seq=seq)
    h = _moe_ffn(h, moe3_w_router.T, bf(moe3_w_gu), bf(moe3_w_down), row(ln_ffn_g[3]), row(ln_ffn_b[3]))
    return h.reshape(batch, seq, d)
```

```python
import functools

import jax
import jax.numpy as jnp
from jax import lax
from jax.experimental import pallas as pl
from jax.experimental.pallas import tpu as pltpu

F32 = jnp.float32
BF16 = jnp.bfloat16
I32 = jnp.int32

D_MODEL = 1024
DEPTH = 4
ALPHA = (2.0 * DEPTH) ** 0.25
LN_EPS = 1e-5
RMS_EPS = 1e-6

A_CHUNK = 128
A_HALF = 2 * D_MODEL
A_GROUPS = 8
A_GROUP_DIM = A_HALF // A_GROUPS

B_HEADS = 16
B_HEAD_DIM = 64
B_IDX_HEADS = 4
B_IDX_DIM = 64
B_TOPK_MAX = 256

C_HEADS = 8
C_EXPAND = 128
C_HEAD_V = 128
C_FDIM = C_HEADS * C_EXPAND
C_SUB = 16
C_HG = 4
C_TS = 512

D_HEADS = 16
D_NOPE = 64
D_ROPE = 32
D_VDIM = 64
D_Q_RANK = 256
D_KV_RANK = 128
D_PAIRS = 2
ROPE_BASE = 10000.0

FFN_DIM = 3584
N_EXPERTS = 8
TOP_K = 2

LANES = 128
VMEM_LIMIT = 52 * 1024 * 1024
NEG = -1e30
LOG2E = 1.4426950408889634
INT_MIN = -2147483648
HIGHEST = lax.Precision.HIGHEST


def _params(*sem):
    return pltpu.CompilerParams(dimension_semantics=sem, vmem_limit_bytes=VMEM_LIMIT)


def _layer_norm(y, g, b):
    mu = jnp.mean(y, axis=-1, keepdims=True)
    d = y - mu
    var = jnp.mean(d * d, axis=-1, keepdims=True)
    return d * lax.rsqrt(var + LN_EPS) * g + b


def _nt_dot(a, b, precision=None):
    return lax.dot_general(a, b, (((1,), (1,)), ((), ())), preferred_element_type=F32, precision=precision)


def _mm_body(a_ref, w_ref, o_ref, *, act, precision):
    a = a_ref[...]
    if precision is None:
        a = a.astype(BF16)
    acc = jnp.dot(a, w_ref[...], preferred_element_type=F32, precision=precision)
    if act == "gelu":
        acc = jax.nn.gelu(acc)
    o_ref[...] = acc.astype(o_ref.dtype)


def _matmul(a, w, *, tm, tn, out_dtype, act=None, precision=None):
    m, k = a.shape
    n = w.shape[1]
    return pl.pallas_call(
        functools.partial(_mm_body, act=act, precision=precision),
        name="matmul",
        grid=(m // tm, n // tn),
        in_specs=[pl.BlockSpec((tm, k), lambda i, j: (i, 0)), pl.BlockSpec((k, tn), lambda i, j: (0, j))],
        out_specs=pl.BlockSpec((tm, tn), lambda i, j: (i, j)),
        out_shape=jax.ShapeDtypeStruct((m, n), out_dtype),
        compiler_params=_params("parallel", "parallel"),
    )(a, w)


def _proj_ln_body(a_ref, w_ref, x_ref, g_ref, b_ref, o_ref):
    h = jnp.dot(a_ref[...].astype(BF16), w_ref[...], preferred_element_type=F32)
    o_ref[...] = _layer_norm(ALPHA * x_ref[...] + h, g_ref[...], b_ref[...])


def _proj_ln(a, w, x, g, b, *, tm=512):
    m, k = a.shape
    return pl.pallas_call(
        _proj_ln_body,
        name="proj_ln",
        grid=(m // tm,),
        in_specs=[pl.BlockSpec((tm, k), lambda i: (i, 0)), pl.BlockSpec((k, D_MODEL), lambda i: (0, 0)),
                  pl.BlockSpec((tm, D_MODEL), lambda i: (i, 0)), pl.BlockSpec((1, D_MODEL), lambda i: (0, 0)),
                  pl.BlockSpec((1, D_MODEL), lambda i: (0, 0))],
        out_specs=pl.BlockSpec((tm, D_MODEL), lambda i: (i, 0)),
        out_shape=jax.ShapeDtypeStruct((m, D_MODEL), F32),
        compiler_params=_params("parallel"),
    )(a, w, x, g, b)


def _ffn_body(x_ref, wg_ref, wu_ref, wd_ref, g_ref, b_ref, o_ref, acc_ref, xb_ref):
    j = pl.program_id(1)

    @pl.when(j == 0)
    def _():
        acc_ref[...] = jnp.zeros_like(acc_ref)
        xb_ref[...] = x_ref[...].astype(BF16)

    xb = xb_ref[...]
    gate = jnp.dot(xb, wg_ref[...], preferred_element_type=F32)
    up = jnp.dot(xb, wu_ref[...], preferred_element_type=F32)
    h = (gate * jax.nn.sigmoid(gate)) * up
    acc_ref[...] += jnp.dot(h.astype(BF16), wd_ref[...], preferred_element_type=F32)

    @pl.when(j == pl.num_programs(1) - 1)
    def _():
        o_ref[...] = _layer_norm(ALPHA * x_ref[...] + acc_ref[...], g_ref[...], b_ref[...])


def _dense_ffn(x, w_gu, w_down, g, b, *, tm=512, tf=512):
    m = x.shape[0]
    nf = FFN_DIM // tf
    return pl.pallas_call(
        _ffn_body,
        name="dense_ffn",
        grid=(m // tm, nf),
        in_specs=[pl.BlockSpec((tm, D_MODEL), lambda i, j: (i, 0)),
                  pl.BlockSpec((D_MODEL, tf), lambda i, j: (0, j)),
                  pl.BlockSpec((D_MODEL, tf), lambda i, j: (0, nf + j)),
                  pl.BlockSpec((tf, D_MODEL), lambda i, j: (j, 0)),
                  pl.BlockSpec((1, D_MODEL), lambda i, j: (0, 0)),
                  pl.BlockSpec((1, D_MODEL), lambda i, j: (0, 0))],
        out_specs=pl.BlockSpec((tm, D_MODEL), lambda i, j: (i, 0)),
        out_shape=jax.ShapeDtypeStruct((m, D_MODEL), F32),
        scratch_shapes=[pltpu.VMEM((tm, D_MODEL), F32), pltpu.VMEM((tm, D_MODEL), BF16)],
        compiler_params=_params("parallel", "arbitrary"),
    )(x, w_gu, w_gu, w_down, g, b)


ROW_TILE = (8, LANES)


def _to_row_tiles(dst_ref, val):
    for k in range(ROW_TILE[0]):
        dst_ref[:, k, :] = val[:, k * LANES:(k + 1) * LANES]


def _from_row_tiles(src_ref, r0, n):
    return jnp.concatenate([src_ref[r0:r0 + n, k, :] for k in range(ROW_TILE[0])], axis=1)


def _router_body(x_ref, wr_ref, e_ref, g_ref, xt_ref):
    _to_row_tiles(xt_ref, x_ref[...])
    logits = _nt_dot(wr_ref[...], x_ref[...], precision=HIGHEST)
    eid = lax.broadcasted_iota(I32, logits.shape, 0).astype(F32)
    m1 = jnp.max(logits, axis=0, keepdims=True)
    i1 = jnp.min(jnp.where(logits == m1, eid, float(N_EXPERTS)), axis=0, keepdims=True)
    rest = jnp.where(eid == i1, -jnp.inf, logits)
    m2 = jnp.max(rest, axis=0, keepdims=True)
    i2 = jnp.min(jnp.where(rest == m2, eid, float(N_EXPERTS)), axis=0, keepdims=True)
    e2 = jnp.exp(m2 - m1)
    den = 1.0 + e2
    e_ref[...] = jnp.concatenate([i1, i2], axis=0).astype(I32)
    g_ref[...] = jnp.concatenate([1.0 / den, e2 / den], axis=0)


def _router(x, w_router_t, *, tm=512):
    m = x.shape[0]
    return pl.pallas_call(
        _router_body,
        name="router",
        grid=(m // tm,),
        in_specs=[pl.BlockSpec((tm, D_MODEL), lambda i: (i, 0)), pl.BlockSpec((N_EXPERTS, D_MODEL), lambda i: (0, 0))],
        out_specs=[pl.BlockSpec((TOP_K, tm), lambda i: (0, i)), pl.BlockSpec((TOP_K, tm), lambda i: (0, i)),
                   pl.BlockSpec((tm,) + ROW_TILE, lambda i: (i, 0, 0))],
        out_shape=[jax.ShapeDtypeStruct((TOP_K, m), I32), jax.ShapeDtypeStruct((TOP_K, m), F32),
                   jax.ShapeDtypeStruct((m,) + ROW_TILE, F32)],
        compiler_params=_params("parallel"),
    )(x, w_router_t)


def _start_row_gather(idx_ref, src_hbm, dst_ref, sem):
    def start(r, c):
        pltpu.make_async_copy(src_hbm.at[pl.ds(idx_ref[0, 0, r], 1)], dst_ref.at[pl.ds(r, 1)], sem).start()
        return c

    lax.fori_loop(0, dst_ref.shape[0], start, 0)


def _wait_row_gather(src_hbm, dst_ref, sem):
    pltpu.make_async_copy(src_hbm.at[pl.ds(0, dst_ref.shape[0])], dst_ref, sem).wait()


def _moe_body(te_ref, tok_ref, nxt_ref, x_hbm, wg_ref, wu_ref, wd_ref, y_ref, xrow_ref, xb_ref, acc_ref, sem):
    del te_ref
    i = pl.program_id(0)
    j = pl.program_id(1)
    tm = xb_ref.shape[0]

    @pl.when(j == 0)
    def _():
        slot = lax.rem(i, 2)

        @pl.when(i == 0)
        def _():
            _start_row_gather(tok_ref, x_hbm, xrow_ref.at[0], sem.at[0])

        @pl.when(i + 1 < pl.num_programs(0))
        def _():
            _start_row_gather(nxt_ref, x_hbm, xrow_ref.at[1 - slot], sem.at[1 - slot])

        _wait_row_gather(x_hbm, xrow_ref.at[slot], sem.at[slot])
        xb_ref[...] = _from_row_tiles(xrow_ref.at[slot], 0, tm).astype(BF16)
        acc_ref[...] = jnp.zeros_like(acc_ref)

    xb = xb_ref[...]
    gate = jnp.dot(xb, wg_ref[...], preferred_element_type=F32)
    up = jnp.dot(xb, wu_ref[...], preferred_element_type=F32)
    h = (gate * jax.nn.sigmoid(gate)) * up
    acc_ref[...] += jnp.dot(h.astype(BF16), wd_ref[...], preferred_element_type=F32)

    @pl.when(j == pl.num_programs(1) - 1)
    def _():
        _to_row_tiles(y_ref, acc_ref[...])


def _combine_body(pos_ref, x_ref, gt_ref, y_hbm, g_ref, b_ref, o_ref, rows_ref, sem):
    tq = x_ref.shape[0]
    _start_row_gather(pos_ref, y_hbm, rows_ref, sem)
    _wait_row_gather(y_hbm, rows_ref, sem)
    gt = gt_ref[...]
    h = gt[:, 0:1] * _from_row_tiles(rows_ref, 0, tq) + gt[:, 1:2] * _from_row_tiles(rows_ref, tq, tq)
    o_ref[...] = _layer_norm(ALPHA * x_ref[...] + h, g_ref[...], b_ref[...])


def _moe_ffn(x, w_router_t, w_gu, w_down, g, b, *, tm=512, tf=512, tq=256):
    m = x.shape[0]
    n_assign = m * TOP_K
    n_tiles = n_assign // tm + N_EXPERTS
    n_rows = n_tiles * tm
    nf = FFN_DIM // tf

    eidx, gate, x_tiles = _router(x, w_router_t)
    flat_e = eidx.reshape(-1)
    onehot = (flat_e[:, None] == jnp.arange(N_EXPERTS, dtype=I32)[None, :]).astype(I32)
    csum = jnp.cumsum(onehot, axis=0)
    rank = jnp.sum(onehot * csum, axis=1) - 1
    counts = csum[-1]
    padded = (counts + tm - 1) // tm * tm
    padded_ends = jnp.cumsum(padded)
    padded_starts = padded_ends - padded
    dest = padded_starts[flat_e] + rank
    tok = jnp.tile(jnp.arange(m, dtype=I32), TOP_K)
    row_tok3 = jnp.zeros((n_rows,), I32).at[dest].set(tok).reshape(n_tiles, 1, tm)
    tile_start = jnp.arange(n_tiles, dtype=I32) * tm
    tile_e = jnp.minimum(jnp.sum((tile_start[:, None] >= padded_ends[None, :]).astype(I32), axis=1), N_EXPERTS - 1)

    y = pl.pallas_call(
        _moe_body,
        name="moe_ffn",
        grid_spec=pltpu.PrefetchScalarGridSpec(
            num_scalar_prefetch=1,
            grid=(n_tiles, nf),
            in_specs=[pl.BlockSpec((1, 1, tm), lambda i, j, te: (i, 0, 0), memory_space=pltpu.SMEM),
                      pl.BlockSpec((1, 1, tm), lambda i, j, te: (jnp.minimum(i + 1, n_tiles - 1), 0, 0),
                                   memory_space=pltpu.SMEM),
                      pl.BlockSpec(memory_space=pl.ANY),
                      pl.BlockSpec((None, D_MODEL, tf), lambda i, j, te: (te[i], 0, j)),
                      pl.BlockSpec((None, D_MODEL, tf), lambda i, j, te: (te[i], 0, nf + j)),
                      pl.BlockSpec((None, tf, D_MODEL), lambda i, j, te: (te[i], j, 0))],
            out_specs=pl.BlockSpec((tm,) + ROW_TILE, lambda i, j, te: (i, 0, 0)),
            scratch_shapes=[pltpu.VMEM((2, tm) + ROW_TILE, F32), pltpu.VMEM((tm, D_MODEL), BF16),
                            pltpu.VMEM((tm, D_MODEL), F32), pltpu.SemaphoreType.DMA((2,))]),
        out_shape=jax.ShapeDtypeStruct((n_rows,) + ROW_TILE, F32),
        compiler_params=_params("arbitrary", "arbitrary"),
    )(tile_e, row_tok3, row_tok3, x_tiles, w_gu, w_gu, w_down)

    pos = jnp.concatenate([dest[:m].reshape(m // tq, 1, tq), dest[m:].reshape(m // tq, 1, tq)], axis=2)
    return pl.pallas_call(
        _combine_body,
        name="moe_combine",
        grid=(m // tq,),
        in_specs=[pl.BlockSpec((1, 1, 2 * tq), lambda i: (i, 0, 0), memory_space=pltpu.SMEM),
                  pl.BlockSpec((tq, D_MODEL), lambda i: (i, 0)),
                  pl.BlockSpec((tq, TOP_K), lambda i: (i, 0)),
                  pl.BlockSpec(memory_space=pl.ANY),
                  pl.BlockSpec((1, D_MODEL), lambda i: (0, 0)),
                  pl.BlockSpec((1, D_MODEL), lambda i: (0, 0))],
        out_specs=pl.BlockSpec((tq, D_MODEL), lambda i: (i, 0)),
        out_shape=jax.ShapeDtypeStruct((m, D_MODEL), F32),
        scratch_shapes=[pltpu.VMEM((2 * tq,) + ROW_TILE, F32), pltpu.SemaphoreType.DMA(())],
        compiler_params=_params("arbitrary"),
    )(pos, x, gate.T, y, g, b)


def _gmlp_gate_body(u_ref, v_ref, lg_ref, lb_ref, ws_ref, bs_ref, o_ref):
    tm = u_ref.shape[0]
    v = _layer_norm(v_ref[...], lg_ref[...], lb_ref[...]).astype(BF16)
    r = lax.broadcasted_iota(I32, (A_CHUNK, A_CHUNK), 0)
    c = lax.broadcasted_iota(I32, (A_CHUNK, A_CHUNK), 1)
    causal = c <= r
    bs = bs_ref[...]
    for grp in range(A_GROUPS):
        wc = jnp.where(causal, ws_ref[grp], 0.0).astype(BF16)
        lo = grp * A_GROUP_DIM
        for ch in range(tm // A_CHUNK):
            r0 = ch * A_CHUNK
            s = jnp.dot(wc, v[r0:r0 + A_CHUNK, lo:lo + A_GROUP_DIM], preferred_element_type=F32)
            s = s + bs[:, grp:grp + 1]
            o_ref[r0:r0 + A_CHUNK, lo:lo + A_GROUP_DIM] = (u_ref[r0:r0 + A_CHUNK, lo:lo + A_GROUP_DIM] * s).astype(BF16)


def _gmlp_mixer(x, w_in, ln_g, ln_b, w_s, b_s_t, w_out, g, b, *, tm=256):
    m = x.shape[0]
    uv = _matmul(x, w_in, tm=512, tn=1024, out_dtype=F32, act="gelu")
    gated = pl.pallas_call(
        _gmlp_gate_body,
        name="gmlp_gate",
        grid=(m // tm,),
        in_specs=[pl.BlockSpec((tm, A_HALF), lambda i: (i, 0)), pl.BlockSpec((tm, A_HALF), lambda i: (i, 1)),
                  pl.BlockSpec((1, A_HALF), lambda i: (0, 0)), pl.BlockSpec((1, A_HALF), lambda i: (0, 0)),
                  pl.BlockSpec((A_GROUPS, A_CHUNK, A_CHUNK), lambda i: (0, 0, 0)),
                  pl.BlockSpec((A_CHUNK, A_GROUPS), lambda i: (0, 0))],
        out_specs=pl.BlockSpec((tm, A_HALF), lambda i: (i, 0)),
        out_shape=jax.ShapeDtypeStruct((m, A_HALF), BF16),
        compiler_params=_params("parallel"),
    )(uv, uv, ln_g, ln_b, w_s, b_s_t)
    return _proj_ln(gated, w_out, x, g, b)


def _flash_pairs(pairs, tq, n_plain, n_kc, ck, bias_fn):
    even = lax.broadcasted_iota(I32, (tq, LANES), 1) < (LANES // 2)
    ones = jnp.ones((ck, LANES), BF16)

    def head(q_ref, q_lane, k_ref, k_lane, c0, bias, v_aug, m):
        s = _nt_dot(q_ref[:, q_lane:q_lane + LANES], k_ref[pl.ds(c0, ck), k_lane:k_lane + LANES])
        if bias is not None:
            s = s + bias
        m_new = jnp.maximum(m, jnp.max(s, axis=1, keepdims=True))
        p = jnp.exp2(s - m_new).astype(BF16)
        pv = jnp.dot(p, v_aug, preferred_element_type=F32)
        return m_new, jnp.exp2(m - m_new), pv

    def step(c, carry, masked):
        c0 = pl.multiple_of(c * ck, ck)
        bias = bias_fn(c0) if masked else None
        new = []
        for (qe_ref, qe_lane, qo_ref, qo_lane, k_ref, kle, klo, v_ref, v_lane), (me, mo, l, acc) in zip(pairs, carry):
            v_aug = jnp.concatenate([v_ref[pl.ds(c0, ck), v_lane:v_lane + LANES], ones], axis=1)
            me, ae, pve = head(qe_ref, qe_lane, k_ref, kle, c0, bias, v_aug, me)
            mo, ao, pvo = head(qo_ref, qo_lane, k_ref, klo, c0, bias, v_aug, mo)
            a = jnp.where(even, ae, ao)
            l = a * l + jnp.where(even, pve[:, LANES:], pvo[:, LANES:])
            acc = a * acc + jnp.where(even, pve[:, :LANES], pvo[:, :LANES])
            new.append((me, mo, l, acc))
        return tuple(new)

    stat = jnp.full((tq, 1), NEG, F32)
    init = tuple((stat, stat, jnp.zeros((tq, LANES), F32), jnp.zeros((tq, LANES), F32)) for _ in pairs)
    carry = lax.fori_loop(0, n_plain, lambda c, cr: step(c, cr, False), init)
    carry = lax.fori_loop(n_plain, n_kc, lambda c, cr: step(c, cr, True), carry)
    return [acc / l for (_, _, l, acc) in carry]


def _split_bf16(v):
    hi = v.astype(BF16)
    return hi, (v - hi.astype(F32)).astype(BF16)


def _dsa_body(q_ref, k_ref, v_ref, qi_ref, ki_ref, wt_ref, o_ref, s_ref, bias_ref, qe_ref, qo_ref, qih_ref, qil_ref,
              jcut_ref, *, tq, ck, topk):
    i = pl.program_id(1)
    sc = 2 * LANES
    n_sc = (i * tq + tq) // sc
    n_kc = (i * tq + tq + ck - 1) // ck
    seq = s_ref.shape[0]
    kpos = lax.broadcasted_iota(I32, (sc, tq), 0)
    qpos = i * tq + lax.broadcasted_iota(I32, (sc, tq), 1)
    lane = lax.broadcasted_iota(I32, (tq, LANES), 1)
    idx_scale = (B_IDX_DIM * B_IDX_HEADS) ** -0.5

    for h in range(B_IDX_HEADS):
        q2 = qi_ref[:, (h // 2) * LANES:(h // 2 + 1) * LANES]
        in_head = (lane >= B_IDX_DIM) if h % 2 else (lane < B_IDX_DIM)
        hi, lo = _split_bf16(jnp.where(in_head, q2, 0.0))
        qih_ref[:, h * LANES:(h + 1) * LANES] = hi
        qil_ref[:, h * LANES:(h + 1) * LANES] = lo

    def score_chunk(c, carry):
        c0 = pl.multiple_of(c * sc, sc)

        @pl.when(c < n_sc)
        def _():
            k_hi, k_lo = _split_bf16(ki_ref[pl.ds(c0, sc), :])
            score = jnp.zeros((sc, tq), F32)
            for h in range(B_IDX_HEADS):
                q_hi = qih_ref[:, h * LANES:(h + 1) * LANES]
                dots = _nt_dot(k_hi, q_hi) + (_nt_dot(k_hi, qil_ref[:, h * LANES:(h + 1) * LANES]) + _nt_dot(k_lo, q_hi))
                score = score + jnp.maximum(dots, 0.0) * wt_ref[h:h + 1, :]
            score = score * idx_scale
            score = jnp.where(score == 0.0, 0.0, score)
            s_ref[pl.ds(c0, sc), :] = jnp.where(c0 + kpos <= qpos, score, -jnp.inf)

        @pl.when(c >= n_sc)
        def _():
            s_ref[pl.ds(c0, sc), :] = jnp.full((sc, tq), -jnp.inf, F32)

        return carry

    lax.fori_loop(0, n_kc * (ck // sc), score_chunk, 0)

    def count(pred):
        def cb(c, acc):
            c0 = pl.multiple_of(c * sc, sc)
            hit = jnp.where(pred(s_ref[pl.ds(c0, sc), :], c0 + kpos), 1.0, 0.0)
            return acc + jnp.sum(hit, axis=0, keepdims=True)

        return lax.fori_loop(0, n_sc, cb, jnp.zeros((1, tq), F32))

    def key_to_f32(key):
        return lax.bitcast_convert_type(key ^ (lax.shift_right_arithmetic(key, 31) & jnp.int32(0x7FFFFFFF)), F32)

    def bit_step(bi, ans):
        cand = ans + lax.shift_left(jnp.int32(1), 31 - bi)
        cand_f = key_to_f32(cand)
        return jnp.where(count(lambda sv, kp: sv >= cand_f) >= float(topk), cand, ans)

    thr = key_to_f32(lax.fori_loop(0, 32, bit_step, jnp.full((1, tq), INT_MIN, I32)))
    few = qpos[0:1, :] < topk

    n_ge = count(lambda sv, kp: sv >= thr)
    need = float(topk) - count(lambda sv, kp: sv > thr)
    jcut_ref[...] = jnp.full(jcut_ref.shape, seq, I32)
    surplus = jnp.where((n_ge > float(topk)) & jnp.logical_not(few), 1.0, 0.0)

    @pl.when(jnp.max(surplus) > 0.0)
    def _():
        nbits = (seq - 1).bit_length()

        def idx_step(bi, ans):
            cand = ans + lax.shift_left(jnp.int32(1), nbits - 1 - bi)
            return jnp.where(count(lambda sv, kp: (sv == thr) & (kp < cand)) < need, cand, ans)

        jcut_ref[...] = jnp.broadcast_to(lax.fori_loop(0, nbits, idx_step, jnp.zeros((1, tq), I32)), jcut_ref.shape)

    thr_sel = jnp.where(few, -jnp.inf, thr)
    jcut = jnp.where(few, -1, jcut_ref[0:1, :])

    def bias_chunk(c, carry):
        c0 = pl.multiple_of(c * sc, sc)
        sv = s_ref[pl.ds(c0, sc), :]
        sel = (sv > thr_sel) | ((sv == thr_sel) & (c0 + kpos <= jcut))
        for r0 in range(0, tq, sc):
            bias_ref[r0:r0 + sc, pl.ds(c0, sc)] = jnp.where(sel, 0.0, NEG)[:, r0:r0 + sc].T
        return carry

    lax.fori_loop(0, n_kc * (ck // sc), bias_chunk, 0)

    q_scale = B_HEAD_DIM ** -0.5 * LOG2E
    for p in range(B_HEADS // 2):
        ls = slice(p * LANES, (p + 1) * LANES)
        q2 = q_ref[:, ls].astype(F32) * q_scale
        qe_ref[:, ls] = jnp.where(lane < B_HEAD_DIM, q2, 0.0).astype(BF16)
        qo_ref[:, ls] = jnp.where(lane >= B_HEAD_DIM, q2, 0.0).astype(BF16)
    bias_fn = lambda c0: bias_ref[:, pl.ds(c0, ck)]
    group = 2
    for p0 in range(0, B_HEADS // 2, group):
        pairs = [(qe_ref, p * LANES, qo_ref, p * LANES, k_ref, p * LANES, p * LANES, v_ref, p * LANES)
                 for p in range(p0, p0 + group)]
        for p, o in zip(range(p0, p0 + group), _flash_pairs(pairs, tq, 0, n_kc, ck, bias_fn)):
            o_ref[:, p * LANES:(p + 1) * LANES] = o.astype(BF16)


def _dsa_mixer(x, w_qkv, w_idx, w_out, g, b, *, batch, seq, tq=256, ck=512):
    m = x.shape[0]
    nq = seq // tq
    hd = B_HEADS * B_HEAD_DIM
    topk = min(B_TOPK_MAX, seq // 4)
    qkv = _matmul(x, w_qkv, tm=512, tn=1024, out_dtype=BF16)
    idx = _matmul(x, w_idx, tm=512, tn=512, out_dtype=F32, precision=HIGHEST)
    w_rows = jnp.pad(idx[:, 3 * LANES:3 * LANES + B_IDX_HEADS].T, ((0, 8 - B_IDX_HEADS), (0, 0)))
    attn = pl.pallas_call(
        functools.partial(_dsa_body, tq=tq, ck=ck, topk=topk),
        name="dsa_attn",
        grid=(batch, nq),
        in_specs=[pl.BlockSpec((tq, hd), lambda bi, i: (bi * nq + i, 0)),
                  pl.BlockSpec((seq, hd), lambda bi, i: (bi, 1)),
                  pl.BlockSpec((seq, hd), lambda bi, i: (bi, 2)),
                  pl.BlockSpec((tq, 2 * LANES), lambda bi, i: (bi * nq + i, 0)),
                  pl.BlockSpec((seq, LANES), lambda bi, i: (bi, 2)),
                  pl.BlockSpec((8, tq), lambda bi, i: (0, bi * nq + i))],
        out_specs=pl.BlockSpec((tq, hd), lambda bi, i: (bi * nq + i, 0)),
        out_shape=jax.ShapeDtypeStruct((m, hd), BF16),
        scratch_shapes=[pltpu.VMEM((seq, tq), F32), pltpu.VMEM((tq, seq), F32), pltpu.VMEM((tq, hd), BF16),
                        pltpu.VMEM((tq, hd), BF16), pltpu.VMEM((tq, B_IDX_HEADS * LANES), BF16),
                        pltpu.VMEM((tq, B_IDX_HEADS * LANES), BF16), pltpu.VMEM((8, tq), I32)],
        compiler_params=_params("parallel", "arbitrary"),
    )(qkv, qkv, qkv, idx, idx, w_rows)
    return _proj_ln(attn, w_out, x, g, b)


def _hgrn2_body(q_ref, f_ref, i_ref, gt_ref, lbl_ref, ng_ref, o_ref, st_ref, *, layer):
    @pl.when(pl.program_id(2) == 0)
    def _():
        st_ref[...] = jnp.zeros_like(st_ref)

    lg = lbl_ref[...]
    e = jnp.exp(lg - jnp.max(lg, axis=0, keepdims=True))
    sm = e / jnp.sum(e, axis=0, keepdims=True)
    lb_all = [sm[0:1]]
    for d in range(1, DEPTH):
        lb_all.append(lb_all[-1] + sm[d:d + 1])
    lb = lb_all[layer] - lb_all[0]
    ng = ng_ref[...]
    rid = lax.broadcasted_iota(I32, (C_SUB, LANES), 0)

    def sub_chunk(j, carry):
        r0 = pl.multiple_of(j * C_SUB, C_SUB)
        for h in range(C_HG):
            ls = slice(h * LANES, (h + 1) * LANES)
            qb = q_ref[pl.ds(r0, C_SUB), ls]
            lbh = lb[:, ls]
            f = lbh + (1.0 - lbh) * jax.nn.sigmoid(f_ref[pl.ds(r0, C_SUB), ls])
            gb = jnp.log(f)
            kb = 1.0 - f
            vb = i_ref[pl.ds(r0, C_SUB), ls]
            gc = gb
            for sh in (1, 2, 4, 8):
                gc = gc + jnp.where(rid >= sh, pltpu.roll(gc, sh, 0), 0.0)
            glast = gc[C_SUB - 1:C_SUB, :]
            st = st_ref[h]
            out = _nt_dot((qb * jnp.exp(gc)).astype(BF16), st.astype(BF16))
            for s in range(C_SUB):
                keep = rid >= s
                dec = jnp.exp(jnp.where(keep, gc - gc[s:s + 1, :], 0.0))
                a = jnp.sum(jnp.where(keep, qb * dec * kb[s:s + 1, :], 0.0), axis=1, keepdims=True)
                out = out + a * vb[s:s + 1, :]
            kdec = kb * jnp.exp(glast - gc)
            kv_t = lax.dot_general(vb.astype(BF16), kdec.astype(BF16), (((0,), (0,)), ((), ())),
                                   preferred_element_type=F32)
            st_ref[h] = st * jnp.exp(glast) + kv_t
            gate = gt_ref[pl.ds(r0, C_SUB), ls]
            o = out * lax.rsqrt(jnp.mean(out * out, axis=1, keepdims=True) + RMS_EPS) * ng
            o_ref[pl.ds(r0, C_SUB), ls] = (o * (gate * jax.nn.sigmoid(gate))).astype(BF16)
        return carry

    lax.fori_loop(0, q_ref.shape[0] // C_SUB, sub_chunk, 0)


def _hgrn2_mixer(x, w_in, lb_logits, norm_g, w_out, g, b, *, batch, seq, layer):
    m = x.shape[0]
    ts = min(C_TS, seq)
    ns = seq // ts
    w = C_HG * LANES
    nhb = C_HEADS // C_HG
    proj = _matmul(x, w_in, tm=512, tn=1024, out_dtype=F32)
    spec = lambda blk: pl.BlockSpec((ts, w), lambda bi, hb, s: (bi * ns + s, blk * nhb + hb))
    o = pl.pallas_call(
        functools.partial(_hgrn2_body, layer=layer),
        name="hgrn2_scan",
        grid=(batch, nhb, ns),
        in_specs=[spec(0), spec(1), spec(2), spec(3),
                  pl.BlockSpec((DEPTH, w), lambda bi, hb, s: (0, hb)),
                  pl.BlockSpec((1, LANES), lambda bi, hb, s: (0, 0))],
        out_specs=pl.BlockSpec((ts, w), lambda bi, hb, s: (bi * ns + s, hb)),
        out_shape=jax.ShapeDtypeStruct((m, D_MODEL), BF16),
        scratch_shapes=[pltpu.VMEM((C_HG, C_HEAD_V, C_EXPAND), F32)],
        compiler_params=_params("parallel", "parallel", "arbitrary"),
    )(proj, proj, proj, proj, lb_logits, norm_g)
    return _proj_ln(o, w_out, x, g, b)


def _mla_prep_body(p_ref, pos_ref, invf_ref, qg_ref, kvg_ref, wq_ref, wk_ref, wv_ref, q_ref, k_ref, v_ref):
    tm = p_ref.shape[0]
    lane = lax.broadcasted_iota(I32, (tm, LANES), 1)
    ang = pos_ref[...].astype(F32) * invf_ref[...]
    cos_t = jnp.cos(ang)
    sin_t = jnp.sin(ang)
    cq = p_ref[:, 0:D_Q_RANK]
    cq = cq * lax.rsqrt(jnp.mean(cq * cq, axis=1, keepdims=True) + RMS_EPS) * qg_ref[...]
    ckv = p_ref[:, D_Q_RANK:D_Q_RANK + D_KV_RANK]
    ckv = ckv * lax.rsqrt(jnp.mean(ckv * ckv, axis=1, keepdims=True) + RMS_EPS) * kvg_ref[...]
    k_rot = p_ref[:, 3 * LANES:4 * LANES] * cos_t + p_ref[:, 4 * LANES:5 * LANES] * sin_t
    q_tab = jnp.where(lane < D_NOPE + D_ROPE, cos_t, sin_t) * ((D_NOPE + D_ROPE) ** -0.5 * LOG2E)
    cqb = cq.astype(BF16)
    ckvb = ckv.astype(BF16)
    for h in range(D_HEADS):
        ls = slice(h * LANES, (h + 1) * LANES)
        qh = jnp.dot(cqb, wq_ref[:, ls], preferred_element_type=F32)
        q_ref[:, ls] = (qh * q_tab).astype(BF16)
        kh = jnp.dot(ckvb, wk_ref[:, ls], preferred_element_type=F32)
        k_ref[:, ls] = (kh + k_rot).astype(BF16)
    v_ref[...] = jnp.dot(ckvb, wv_ref[...], preferred_element_type=F32).astype(BF16)


def _mla_attn_body(q_ref, k_ref, v_ref, o_ref, *, tq, ck):
    i = pl.program_id(2)
    n_kc = (i * tq + tq + ck - 1) // ck
    n_plain = (i * tq + 1) // ck
    qpos = i * tq + lax.broadcasted_iota(I32, (tq, ck), 0)
    kpos = lax.broadcasted_iota(I32, (tq, ck), 1)
    bias_fn = lambda c0: jnp.where(c0 + kpos <= qpos, 0.0, NEG)
    pairs = [(q_ref, 2 * p * LANES, q_ref, (2 * p + 1) * LANES, k_ref, 2 * p * LANES, (2 * p + 1) * LANES, v_ref, p * LANES)
             for p in range(D_PAIRS)]
    for p, o in enumerate(_flash_pairs(pairs, tq, n_plain, n_kc, ck, bias_fn)):
        o_ref[:, p * LANES:(p + 1) * LANES] = o.astype(BF16)


def _mla_mixer(x, positions, w_in, invf, q_norm_g, kv_norm_g, w_q, w_k, w_v, w_out, g, b, *, batch, seq, tm=256, tq=256, ck=512):
    m = x.shape[0]
    nq = seq // tq
    hw = D_HEADS * LANES
    proj = _matmul(x, w_in, tm=512, tn=w_in.shape[1], out_dtype=F32)
    hv = D_HEADS * D_VDIM
    q, k, v = pl.pallas_call(
        _mla_prep_body,
        name="mla_prep",
        grid=(m // tm,),
        in_specs=[pl.BlockSpec((tm, w_in.shape[1]), lambda i: (i, 0)), pl.BlockSpec((tm, 1), lambda i: (i, 0)),
                  pl.BlockSpec((1, LANES), lambda i: (0, 0)), pl.BlockSpec((1, D_Q_RANK), lambda i: (0, 0)),
                  pl.BlockSpec((1, D_KV_RANK), lambda i: (0, 0)), pl.BlockSpec((D_Q_RANK, hw), lambda i: (0, 0)),
                  pl.BlockSpec((D_KV_RANK, hw), lambda i: (0, 0)),
                  pl.BlockSpec((D_KV_RANK, hv), lambda i: (0, 0))],
        out_specs=[pl.BlockSpec((tm, hw), lambda i: (i, 0)), pl.BlockSpec((tm, hw), lambda i: (i, 0)),
                   pl.BlockSpec((tm, hv), lambda i: (i, 0))],
        out_shape=[jax.ShapeDtypeStruct((m, hw), BF16), jax.ShapeDtypeStruct((m, hw), BF16),
                   jax.ShapeDtypeStruct((m, hv), BF16)],
        compiler_params=_params("parallel"),
    )(proj, positions.reshape(m, 1), invf, q_norm_g, kv_norm_g, w_q, w_k, w_v)
    attn = pl.pallas_call(
        functools.partial(_mla_attn_body, tq=tq, ck=ck),
        name="mla_attn",
        grid=(batch, D_HEADS // (2 * D_PAIRS), nq),
        in_specs=[pl.BlockSpec((tq, 2 * D_PAIRS * LANES), lambda bi, hp, i: (bi * nq + i, hp)),
                  pl.BlockSpec((seq, 2 * D_PAIRS * LANES), lambda bi, hp, i: (bi, hp)),
                  pl.BlockSpec((seq, D_PAIRS * LANES), lambda bi, hp, i: (bi, hp))],
        out_specs=pl.BlockSpec((tq, D_PAIRS * LANES), lambda bi, hp, i: (bi * nq + i, hp)),
        out_shape=jax.ShapeDtypeStruct((m, hv), BF16),
        compiler_params=_params("parallel", "parallel", "arbitrary"),
    )(q, k, v)
    return _proj_ln(attn, w_out, x, g, b)


def _rotate_half_cols(w):
    half = w.shape[-1] // 2
    return jnp.concatenate([-w[..., half:], w[..., :half]], axis=-1)


def _mla_weights(d_w_in, d_w_uq, d_w_ukv):
    zeros = lambda *s: jnp.zeros(s, F32)
    w_kr = d_w_in[:, D_Q_RANK + D_KV_RANK:]
    w_krh = _rotate_half_cols(w_kr)
    z64 = zeros(D_MODEL, D_NOPE)
    w_in = jnp.concatenate([d_w_in[:, :D_Q_RANK + D_KV_RANK], z64, w_kr, w_kr, z64, w_krh, w_krh], axis=1)
    uq = d_w_uq.reshape(D_Q_RANK, D_HEADS, D_NOPE + D_ROPE)
    w_q = jnp.concatenate([uq, _rotate_half_cols(uq[..., D_NOPE:])], axis=-1).reshape(D_Q_RANK, D_HEADS * LANES)
    ukv = d_w_ukv.reshape(D_KV_RANK, D_HEADS, D_NOPE + D_VDIM)
    w_k = jnp.concatenate([ukv[..., :D_NOPE], zeros(D_KV_RANK, D_HEADS, LANES - D_NOPE)], axis=-1)
    w_k = w_k.reshape(D_KV_RANK, D_HEADS * LANES)
    w_v = ukv[..., D_NOPE:].reshape(D_KV_RANK, D_HEADS * D_VDIM)
    half = D_ROPE // 2
    inv_freq = ROPE_BASE ** (-jnp.arange(half, dtype=F32) / half)
    invf = jnp.concatenate([jnp.zeros((D_NOPE,), F32), jnp.tile(inv_freq, 4)]).reshape(1, LANES)
    return w_in.astype(BF16), w_q.astype(BF16), w_k.astype(BF16), w_v.astype(BF16), invf


def _dsa_weights(b_w_in):
    hd = B_HEADS * B_HEAD_DIM
    o = 3 * hd
    nqi = B_IDX_HEADS * B_IDX_DIM
    w_qi = b_w_in[:, o:o + nqi]
    w_ki = b_w_in[:, o + nqi:o + nqi + B_IDX_DIM]
    w_wi = b_w_in[:, o + nqi + B_IDX_DIM:]
    pad = jnp.zeros((D_MODEL, LANES - B_IDX_HEADS), F32)
    w_idx = jnp.concatenate([w_qi, w_ki, w_ki, w_wi, pad], axis=1)
    return b_w_in[:, :o].astype(BF16), w_idx


def kernel(x, positions, a_w_in, a_ln_g, a_ln_b, a_w_s, a_b_s, a_w_out, b_w_in, b_w_out, c_w_in, c_lb_logits, c_norm_g, c_w_out, d_w_in, d_q_norm_g, d_w_uq, d_kv_norm_g, d_w_ukv, d_w_out, ffn0_w_gu, ffn0_w_down, moe1_w_router, moe1_w_gu, moe1_w_down, ffn2_w_gu, ffn2_w_down, moe3_w_router, moe3_w_gu, moe3_w_down, ln_mix_g, ln_mix_b, ln_ffn_g, ln_ffn_b):
    batch, seq, d = x.shape
    m = batch * seq
    row = lambda v: v.reshape(1, -1).astype(F32)
    bf = lambda w: w.astype(BF16)
    h = x.reshape(m, d)

    h = _gmlp_mixer(h, bf(a_w_in), row(a_ln_g), row(a_ln_b), a_w_s, a_b_s.T, bf(a_w_out),
                    row(ln_mix_g[0]), row(ln_mix_b[0]))
    h = _dense_ffn(h, bf(ffn0_w_gu), bf(ffn0_w_down), row(ln_ffn_g[0]), row(ln_ffn_b[0]))

    w_qkv, w_idx = _dsa_weights(b_w_in)
    h = _dsa_mixer(h, w_qkv, w_idx, bf(b_w_out), row(ln_mix_g[1]), row(ln_mix_b[1]), batch=batch, seq=seq)
    h = _moe_ffn(h, moe1_w_router.T, bf(moe1_w_gu), bf(moe1_w_down), row(ln_ffn_g[1]), row(ln_ffn_b[1]))

    h = _hgrn2_mixer(h, bf(c_w_in), c_lb_logits, row(c_norm_g), bf(c_w_out), row(ln_mix_g[2]), row(ln_mix_b[2]),
                     batch=batch, seq=seq, layer=2)
    h = _dense_ffn(h, bf(ffn2_w_gu), bf(ffn2_w_down), row(ln_ffn_g[2]), row(ln_ffn_b[2]))

    w_in, w_q, w_k, w_v, invf = _mla_weights(d_w_in, d_w_uq, d_w_ukv)
    h = _mla_mixer(h, positions, w_in, invf, row(d_q_norm_g), row(d_kv_norm_g), w_q, w_k, w_v, bf(d_w_out),
                   row(ln_mix_g[3]), row(ln_mix_b[3]), batch=batch, seq=seq)
    h = _moe_ffn(h, moe3_w_router.T, bf(moe3_w_gu), bf(moe3_w_down), row(ln_ffn_g[3]), row(ln_ffn_b[3]))
    return h.reshape(batch, seq, d)
```

```python
import functools

import jax
import jax.numpy as jnp
from jax import lax
from jax.experimental import pallas as pl
from jax.experimental.pallas import tpu as pltpu

F32 = jnp.float32
BF16 = jnp.bfloat16
I32 = jnp.int32

D_MODEL = 1024
DEPTH = 4
ALPHA = (2.0 * DEPTH) ** 0.25
LN_EPS = 1e-5
RMS_EPS = 1e-6

A_CHUNK = 128
A_HALF = 2 * D_MODEL
A_GROUPS = 8
A_GROUP_DIM = A_HALF // A_GROUPS

B_HEADS = 16
B_HEAD_DIM = 64
B_IDX_HEADS = 4
B_IDX_DIM = 64
B_TOPK_MAX = 256

C_HEADS = 8
C_EXPAND = 128
C_HEAD_V = 128
C_FDIM = C_HEADS * C_EXPAND
C_SUB = 16
C_HG = 4
C_TS = 512

D_HEADS = 16
D_NOPE = 64
D_ROPE = 32
D_VDIM = 64
D_Q_RANK = 256
D_KV_RANK = 128
D_PAIRS = 2
ROPE_BASE = 10000.0

FFN_DIM = 3584
N_EXPERTS = 8
TOP_K = 2

LANES = 128
VMEM_LIMIT = 56 * 1024 * 1024
NEG = -1e30
LOG2E = 1.4426950408889634
INT_MIN = -2147483648
HIGHEST = lax.Precision.HIGHEST


def _params(*sem):
    return pltpu.CompilerParams(dimension_semantics=sem, vmem_limit_bytes=VMEM_LIMIT)


def _layer_norm(y, g, b):
    mu = jnp.mean(y, axis=-1, keepdims=True)
    d = y - mu
    var = jnp.mean(d * d, axis=-1, keepdims=True)
    return d * lax.rsqrt(var + LN_EPS) * g + b


def _nt_dot(a, b, precision=None):
    return lax.dot_general(a, b, (((1,), (1,)), ((), ())), preferred_element_type=F32, precision=precision)


def _mm_body(a_ref, w_ref, o_ref, *, act, precision):
    a = a_ref[...]
    if precision is None:
        a = a.astype(BF16)
    acc = jnp.dot(a, w_ref[...], preferred_element_type=F32, precision=precision)
    if act == "gelu":
        acc = jax.nn.gelu(acc)
    o_ref[...] = acc.astype(o_ref.dtype)


def _matmul(a, w, *, tm, tn, out_dtype, act=None, precision=None):
    m, k = a.shape
    n = w.shape[1]
    return pl.pallas_call(
        functools.partial(_mm_body, act=act, precision=precision),
        name="matmul",
        grid=(m // tm, n // tn),
        in_specs=[pl.BlockSpec((tm, k), lambda i, j: (i, 0)), pl.BlockSpec((k, tn), lambda i, j: (0, j))],
        out_specs=pl.BlockSpec((tm, tn), lambda i, j: (i, j)),
        out_shape=jax.ShapeDtypeStruct((m, n), out_dtype),
        compiler_params=_params("parallel", "parallel"),
    )(a, w)


def _proj_ln_body(a_ref, w_ref, x_ref, g_ref, b_ref, o_ref):
    h = jnp.dot(a_ref[...].astype(BF16), w_ref[...], preferred_element_type=F32)
    o_ref[...] = _layer_norm(ALPHA * x_ref[...] + h, g_ref[...], b_ref[...])


def _proj_ln(a, w, x, g, b, *, tm=512):
    m, k = a.shape
    return pl.pallas_call(
        _proj_ln_body,
        name="proj_ln",
        grid=(m // tm,),
        in_specs=[pl.BlockSpec((tm, k), lambda i: (i, 0)), pl.BlockSpec((k, D_MODEL), lambda i: (0, 0)),
                  pl.BlockSpec((tm, D_MODEL), lambda i: (i, 0)), pl.BlockSpec((1, D_MODEL), lambda i: (0, 0)),
                  pl.BlockSpec((1, D_MODEL), lambda i: (0, 0))],
        out_specs=pl.BlockSpec((tm, D_MODEL), lambda i: (i, 0)),
        out_shape=jax.ShapeDtypeStruct((m, D_MODEL), F32),
        compiler_params=_params("parallel"),
    )(a, w, x, g, b)


def _ffn_body(x_ref, wg_ref, wu_ref, wd_ref, g_ref, b_ref, o_ref, acc_ref, xb_ref):
    j = pl.program_id(1)

    @pl.when(j == 0)
    def _():
        acc_ref[...] = jnp.zeros_like(acc_ref)
        xb_ref[...] = x_ref[...].astype(BF16)

    xb = xb_ref[...]
    gate = jnp.dot(xb, wg_ref[...].astype(BF16), preferred_element_type=F32)
    up = jnp.dot(xb, wu_ref[...].astype(BF16), preferred_element_type=F32)
    h = (gate * jax.nn.sigmoid(gate)) * up
    acc_ref[...] += jnp.dot(h.astype(BF16), wd_ref[...].astype(BF16), preferred_element_type=F32)

    @pl.when(j == pl.num_programs(1) - 1)
    def _():
        o_ref[...] = _layer_norm(ALPHA * x_ref[...] + acc_ref[...], g_ref[...], b_ref[...])


def _dense_ffn(x, w_gu, w_down, g, b, *, tm=1024, tf=512):
    m = x.shape[0]
    nf = FFN_DIM // tf
    return pl.pallas_call(
        _ffn_body,
        name="dense_ffn",
        grid=(m // tm, nf),
        in_specs=[pl.BlockSpec((tm, D_MODEL), lambda i, j: (i, 0)),
                  pl.BlockSpec((D_MODEL, tf), lambda i, j: (0, j)),
                  pl.BlockSpec((D_MODEL, tf), lambda i, j: (0, nf + j)),
                  pl.BlockSpec((tf, D_MODEL), lambda i, j: (j, 0)),
                  pl.BlockSpec((1, D_MODEL), lambda i, j: (0, 0)),
                  pl.BlockSpec((1, D_MODEL), lambda i, j: (0, 0))],
        out_specs=pl.BlockSpec((tm, D_MODEL), lambda i, j: (i, 0)),
        out_shape=jax.ShapeDtypeStruct((m, D_MODEL), F32),
        scratch_shapes=[pltpu.VMEM((tm, D_MODEL), F32), pltpu.VMEM((tm, D_MODEL), BF16)],
        compiler_params=_params("parallel", "arbitrary"),
    )(x, w_gu, w_gu, w_down, g, b)


ROW_TILE = (8, LANES)


def _to_row_tiles(dst_ref, val):
    for k in range(ROW_TILE[0]):
        dst_ref[:, k, :] = val[:, k * LANES:(k + 1) * LANES]


def _from_row_tiles(src_ref, r0, n):
    return jnp.concatenate([src_ref[r0:r0 + n, k, :] for k in range(ROW_TILE[0])], axis=1)


def _router_body(x_ref, wr_ref, e_ref, g_ref, xt_ref):
    _to_row_tiles(xt_ref, x_ref[...])
    logits = _nt_dot(wr_ref[...], x_ref[...], precision=HIGHEST)
    eid = lax.broadcasted_iota(I32, logits.shape, 0).astype(F32)
    m1 = jnp.max(logits, axis=0, keepdims=True)
    i1 = jnp.min(jnp.where(logits == m1, eid, float(N_EXPERTS)), axis=0, keepdims=True)
    rest = jnp.where(eid == i1, -jnp.inf, logits)
    m2 = jnp.max(rest, axis=0, keepdims=True)
    i2 = jnp.min(jnp.where(rest == m2, eid, float(N_EXPERTS)), axis=0, keepdims=True)
    e2 = jnp.exp(m2 - m1)
    den = 1.0 + e2
    e_ref[...] = jnp.concatenate([i1, i2], axis=0).astype(I32)
    g_ref[...] = jnp.concatenate([1.0 / den, e2 / den], axis=0)


def _router(x, w_router_t, *, tm=512):
    m = x.shape[0]
    return pl.pallas_call(
        _router_body,
        name="router",
        grid=(m // tm,),
        in_specs=[pl.BlockSpec((tm, D_MODEL), lambda i: (i, 0)), pl.BlockSpec((N_EXPERTS, D_MODEL), lambda i: (0, 0))],
        out_specs=[pl.BlockSpec((TOP_K, tm), lambda i: (0, i)), pl.BlockSpec((TOP_K, tm), lambda i: (0, i)),
                   pl.BlockSpec((tm,) + ROW_TILE, lambda i: (i, 0, 0))],
        out_shape=[jax.ShapeDtypeStruct((TOP_K, m), I32), jax.ShapeDtypeStruct((TOP_K, m), F32),
                   jax.ShapeDtypeStruct((m,) + ROW_TILE, F32)],
        compiler_params=_params("parallel"),
    )(x, w_router_t)


def _start_row_gather(idx_ref, src_hbm, dst_ref, sem):
    def start(r, c):
        pltpu.make_async_copy(src_hbm.at[pl.ds(idx_ref[0, 0, r], 1)], dst_ref.at[pl.ds(r, 1)], sem).start()
        return c

    lax.fori_loop(0, dst_ref.shape[0], start, 0, unroll=8)


def _wait_row_gather(src_hbm, dst_ref, sem):
    pltpu.make_async_copy(src_hbm.at[pl.ds(0, dst_ref.shape[0])], dst_ref, sem).wait()


def _moe_body(te_ref, nu_ref, tok_ref, nxt_ref, x_hbm, wg_ref, wu_ref, wd_ref, y_ref, xrow_ref, xb_ref, acc_ref, sem):
    del te_ref
    i = pl.program_id(0)
    j = pl.program_id(1)
    tm = xb_ref.shape[0]
    n_used = nu_ref[0]

    @pl.when((j == 0) & (i < n_used))
    def _():
        slot = lax.rem(i, 2)

        @pl.when(i == 0)
        def _():
            _start_row_gather(tok_ref, x_hbm, xrow_ref.at[0], sem.at[0])

        @pl.when(i + 1 < n_used)
        def _():
            _start_row_gather(nxt_ref, x_hbm, xrow_ref.at[1 - slot], sem.at[1 - slot])

        _wait_row_gather(x_hbm, xrow_ref.at[slot], sem.at[slot])
        xb_ref[...] = _from_row_tiles(xrow_ref.at[slot], 0, tm).astype(BF16)
        acc_ref[...] = jnp.zeros_like(acc_ref)

    @pl.when(i < n_used)
    def _():
        xb = xb_ref[...]
        gate = jnp.dot(xb, wg_ref[...].astype(BF16), preferred_element_type=F32)
        up = jnp.dot(xb, wu_ref[...].astype(BF16), preferred_element_type=F32)
        h = (gate * jax.nn.sigmoid(gate)) * up
        acc_ref[...] += jnp.dot(h.astype(BF16), wd_ref[...].astype(BF16), preferred_element_type=F32)

    @pl.when(j == pl.num_programs(1) - 1)
    def _():
        @pl.when(i < n_used)
        def _():
            _to_row_tiles(y_ref, acc_ref[...])

        @pl.when(i >= n_used)
        def _():
            y_ref[...] = jnp.zeros_like(y_ref)


def _combine_body(pos_ref, nxt_ref, x_ref, gt_ref, y_hbm, g_ref, b_ref, o_ref, rows_ref, sem):
    i = pl.program_id(0)
    tq = x_ref.shape[0]
    slot = lax.rem(i, 2)

    @pl.when(i == 0)
    def _():
        _start_row_gather(pos_ref, y_hbm, rows_ref.at[0], sem.at[0])

    @pl.when(i + 1 < pl.num_programs(0))
    def _():
        _start_row_gather(nxt_ref, y_hbm, rows_ref.at[1 - slot], sem.at[1 - slot])

    _wait_row_gather(y_hbm, rows_ref.at[slot], sem.at[slot])
    rows = rows_ref.at[slot]
    gt = gt_ref[...]
    h = gt[:, 0:1] * _from_row_tiles(rows, 0, tq) + gt[:, 1:2] * _from_row_tiles(rows, tq, tq)
    o_ref[...] = _layer_norm(ALPHA * x_ref[...] + h, g_ref[...], b_ref[...])


def _moe_ffn(x, w_router_t, w_gu, w_down, g, b, *, tm=1024, tf=512, tq=256):
    m = x.shape[0]
    n_assign = m * TOP_K
    n_tiles = n_assign // tm + N_EXPERTS
    n_rows = n_tiles * tm
    nf = FFN_DIM // tf

    eidx, gate, x_tiles = _router(x, w_router_t)
    flat_e = eidx.reshape(-1)
    onehot = (flat_e[:, None] == jnp.arange(N_EXPERTS, dtype=I32)[None, :]).astype(I32)
    csum = jnp.cumsum(onehot, axis=0)
    rank = jnp.sum(onehot * csum, axis=1) - 1
    counts = csum[-1]
    padded = (counts + tm - 1) // tm * tm
    padded_ends = jnp.cumsum(padded)
    padded_starts = padded_ends - padded
    dest = padded_starts[flat_e] + rank
    tok = jnp.tile(jnp.arange(m, dtype=I32), TOP_K)
    row_tok3 = jnp.zeros((n_rows,), I32).at[dest].set(tok).reshape(n_tiles, 1, tm)
    n_used = (padded_ends[-1:] // tm).astype(I32)
    w_step = lambda i, j, nu: jnp.where(i < nu[0], j, nf - 1)
    tile_start = jnp.arange(n_tiles, dtype=I32) * tm
    tile_e = jnp.minimum(jnp.sum((tile_start[:, None] >= padded_ends[None, :]).astype(I32), axis=1), N_EXPERTS - 1)

    y = pl.pallas_call(
        _moe_body,
        name="moe_ffn",
        grid_spec=pltpu.PrefetchScalarGridSpec(
            num_scalar_prefetch=2,
            grid=(n_tiles, nf),
            in_specs=[pl.BlockSpec((1, 1, tm), lambda i, j, te, nu: (i, 0, 0), memory_space=pltpu.SMEM),
                      pl.BlockSpec((1, 1, tm), lambda i, j, te, nu: (jnp.minimum(i + 1, n_tiles - 1), 0, 0),
                                   memory_space=pltpu.SMEM),
                      pl.BlockSpec(memory_space=pl.ANY),
                      pl.BlockSpec((None, D_MODEL, tf), lambda i, j, te, nu: (te[i], 0, w_step(i, j, nu))),
                      pl.BlockSpec((None, D_MODEL, tf), lambda i, j, te, nu: (te[i], 0, nf + w_step(i, j, nu))),
                      pl.BlockSpec((None, tf, D_MODEL), lambda i, j, te, nu: (te[i], w_step(i, j, nu), 0))],
            out_specs=pl.BlockSpec((tm,) + ROW_TILE, lambda i, j, te, nu: (i, 0, 0)),
            scratch_shapes=[pltpu.VMEM((2, tm) + ROW_TILE, F32), pltpu.VMEM((tm, D_MODEL), BF16),
                            pltpu.VMEM((tm, D_MODEL), F32), pltpu.SemaphoreType.DMA((2,))]),
        out_shape=jax.ShapeDtypeStruct((n_rows,) + ROW_TILE, F32),
        compiler_params=_params("arbitrary", "arbitrary"),
    )(tile_e, n_used, row_tok3, row_tok3, x_tiles, w_gu, w_gu, w_down)

    pos = jnp.concatenate([dest[:m].reshape(m // tq, 1, tq), dest[m:].reshape(m // tq, 1, tq)], axis=2)
    return pl.pallas_call(
        _combine_body,
        name="moe_combine",
        grid=(m // tq,),
        in_specs=[pl.BlockSpec((1, 1, 2 * tq), lambda i: (i, 0, 0), memory_space=pltpu.SMEM),
                  pl.BlockSpec((1, 1, 2 * tq), lambda i: (jnp.minimum(i + 1, m // tq - 1), 0, 0),
                               memory_space=pltpu.SMEM),
                  pl.BlockSpec((tq, D_MODEL), lambda i: (i, 0)),
                  pl.BlockSpec((tq, TOP_K), lambda i: (i, 0)),
                  pl.BlockSpec(memory_space=pl.ANY),
                  pl.BlockSpec((1, D_MODEL), lambda i: (0, 0)),
                  pl.BlockSpec((1, D_MODEL), lambda i: (0, 0))],
        out_specs=pl.BlockSpec((tq, D_MODEL), lambda i: (i, 0)),
        out_shape=jax.ShapeDtypeStruct((m, D_MODEL), F32),
        scratch_shapes=[pltpu.VMEM((2, 2 * tq) + ROW_TILE, F32), pltpu.SemaphoreType.DMA((2,))],
        compiler_params=_params("arbitrary"),
    )(pos, pos, x, gate.T, y, g, b)


def _gmlp_gate_body(u_ref, v_ref, lg_ref, lb_ref, ws_ref, bs_ref, o_ref):
    tm = u_ref.shape[0]
    v = _layer_norm(v_ref[...], lg_ref[...], lb_ref[...]).astype(BF16)
    r = lax.broadcasted_iota(I32, (A_CHUNK, A_CHUNK), 0)
    c = lax.broadcasted_iota(I32, (A_CHUNK, A_CHUNK), 1)
    causal = c <= r
    bs = bs_ref[...]
    for grp in range(A_GROUPS):
        wc = jnp.where(causal, ws_ref[grp], 0.0).astype(BF16)
        lo = grp * A_GROUP_DIM
        for ch in range(tm // A_CHUNK):
            r0 = ch * A_CHUNK
            s = jnp.dot(wc, v[r0:r0 + A_CHUNK, lo:lo + A_GROUP_DIM], preferred_element_type=F32)
            s = s + bs[:, grp:grp + 1]
            o_ref[r0:r0 + A_CHUNK, lo:lo + A_GROUP_DIM] = (u_ref[r0:r0 + A_CHUNK, lo:lo + A_GROUP_DIM] * s).astype(BF16)


def _gmlp_mixer(x, w_in, ln_g, ln_b, w_s, b_s_t, w_out, g, b, *, tm=256):
    m = x.shape[0]
    uv = _matmul(x, w_in, tm=512, tn=1024, out_dtype=F32, act="gelu")
    gated = pl.pallas_call(
        _gmlp_gate_body,
        name="gmlp_gate",
        grid=(m // tm,),
        in_specs=[pl.BlockSpec((tm, A_HALF), lambda i: (i, 0)), pl.BlockSpec((tm, A_HALF), lambda i: (i, 1)),
                  pl.BlockSpec((1, A_HALF), lambda i: (0, 0)), pl.BlockSpec((1, A_HALF), lambda i: (0, 0)),
                  pl.BlockSpec((A_GROUPS, A_CHUNK, A_CHUNK), lambda i: (0, 0, 0)),
                  pl.BlockSpec((A_CHUNK, A_GROUPS), lambda i: (0, 0))],
        out_specs=pl.BlockSpec((tm, A_HALF), lambda i: (i, 0)),
        out_shape=jax.ShapeDtypeStruct((m, A_HALF), BF16),
        compiler_params=_params("parallel"),
    )(uv, uv, ln_g, ln_b, w_s, b_s_t)
    return _proj_ln(gated, w_out, x, g, b)


def _flash_pairs(pairs, tq, n_plain, n_kc, ck, bias_fn):
    even = lax.broadcasted_iota(I32, (tq, LANES), 1) < (LANES // 2)
    ones = jnp.ones((ck, LANES), BF16)

    def head(q_ref, q_lane, k_ref, k_lane, c0, bias, v_aug, m):
        s = _nt_dot(q_ref[:, q_lane:q_lane + LANES], k_ref[pl.ds(c0, ck), k_lane:k_lane + LANES])
        if bias is not None:
            s = s + bias
        m_new = jnp.maximum(m, jnp.max(s, axis=1, keepdims=True))
        p = jnp.exp2(s - m_new).astype(BF16)
        pv = jnp.dot(p, v_aug, preferred_element_type=F32)
        return m_new, jnp.exp2(m - m_new), pv

    def step(c, carry, masked):
        c0 = pl.multiple_of(c * ck, ck)
        bias = bias_fn(c0) if masked else None
        new = []
        for (qe_ref, qe_lane, qo_ref, qo_lane, k_ref, kle, klo, v_ref, v_lane), (me, mo, l, acc) in zip(pairs, carry):
            v_aug = jnp.concatenate([v_ref[pl.ds(c0, ck), v_lane:v_lane + LANES], ones], axis=1)
            me, ae, pve = head(qe_ref, qe_lane, k_ref, kle, c0, bias, v_aug, me)
            mo, ao, pvo = head(qo_ref, qo_lane, k_ref, klo, c0, bias, v_aug, mo)
            a = jnp.where(even, ae, ao)
            l = a * l + jnp.where(even, pve[:, LANES:], pvo[:, LANES:])
            acc = a * acc + jnp.where(even, pve[:, :LANES], pvo[:, :LANES])
            new.append((me, mo, l, acc))
        return tuple(new)

    stat = jnp.full((tq, 1), NEG, F32)
    init = tuple((stat, stat, jnp.zeros((tq, LANES), F32), jnp.zeros((tq, LANES), F32)) for _ in pairs)
    carry = lax.fori_loop(0, n_plain, lambda c, cr: step(c, cr, False), init)
    carry = lax.fori_loop(n_plain, n_kc, lambda c, cr: step(c, cr, True), carry)
    return [acc / l for (_, _, l, acc) in carry]


def _split_bf16(v):
    hi = v.astype(BF16)
    return hi, (v - hi.astype(F32)).astype(BF16)


def _dsa_body(q_ref, k_ref, v_ref, qi_ref, ki_ref, wt_ref, o_ref, s_ref, bias_ref, qe_ref, qo_ref, qih_ref, qil_ref,
              jcut_ref, *, tq, ck, topk):
    i = pl.program_id(1)
    sc = 2 * LANES
    n_sc = (i * tq + tq) // sc
    n_kc = (i * tq + tq + ck - 1) // ck
    seq = s_ref.shape[0]
    kpos = lax.broadcasted_iota(I32, (sc, tq), 0)
    qpos = i * tq + lax.broadcasted_iota(I32, (sc, tq), 1)
    lane = lax.broadcasted_iota(I32, (tq, LANES), 1)
    idx_scale = (B_IDX_DIM * B_IDX_HEADS) ** -0.5

    for h in range(B_IDX_HEADS):
        q2 = qi_ref[:, (h // 2) * LANES:(h // 2 + 1) * LANES]
        in_head = (lane >= B_IDX_DIM) if h % 2 else (lane < B_IDX_DIM)
        hi, lo = _split_bf16(jnp.where(in_head, q2, 0.0))
        qih_ref[:, h * LANES:(h + 1) * LANES] = hi
        qil_ref[:, h * LANES:(h + 1) * LANES] = lo

    def score_chunk(c, carry):
        c0 = pl.multiple_of(c * sc, sc)

        @pl.when(c < n_sc)
        def _():
            k_hi, k_lo = _split_bf16(ki_ref[pl.ds(c0, sc), :])
            score = jnp.zeros((sc, tq), F32)
            for h in range(B_IDX_HEADS):
                q_hi = qih_ref[:, h * LANES:(h + 1) * LANES]
                dots = _nt_dot(k_hi, q_hi) + (_nt_dot(k_hi, qil_ref[:, h * LANES:(h + 1) * LANES]) + _nt_dot(k_lo, q_hi))
                score = score + jnp.maximum(dots, 0.0) * wt_ref[h:h + 1, :]
            score = score * idx_scale
            score = jnp.where(score == 0.0, 0.0, score)
            s_ref[pl.ds(c0, sc), :] = jnp.where(c0 + kpos <= qpos, score, -jnp.inf)

        @pl.when(c >= n_sc)
        def _():
            s_ref[pl.ds(c0, sc), :] = jnp.full((sc, tq), -jnp.inf, F32)

        return carry

    lax.fori_loop(0, n_kc * (ck // sc), score_chunk, 0)

    def count(pred):
        def cb(c, acc):
            c0 = pl.multiple_of(c * sc, sc)
            hit = jnp.where(pred(s_ref[pl.ds(c0, sc), :], c0 + kpos), 1.0, 0.0)
            return acc + jnp.sum(hit, axis=0, keepdims=True)

        return lax.fori_loop(0, n_sc, cb, jnp.zeros((1, tq), F32))

    def key_to_f32(key):
        return lax.bitcast_convert_type(key ^ (lax.shift_right_arithmetic(key, 31) & jnp.int32(0x7FFFFFFF)), F32)

    def bit_step(bi, ans):
        cand = ans + lax.shift_left(jnp.int32(1), 31 - bi)
        cand_f = key_to_f32(cand)
        return jnp.where(count(lambda sv, kp: sv >= cand_f) >= float(topk), cand, ans)

    thr = key_to_f32(lax.fori_loop(0, 32, bit_step, jnp.full((1, tq), INT_MIN, I32)))
    few = qpos[0:1, :] < topk

    n_ge = count(lambda sv, kp: sv >= thr)
    need = float(topk) - count(lambda sv, kp: sv > thr)
    jcut_ref[...] = jnp.full(jcut_ref.shape, seq, I32)
    surplus = jnp.where((n_ge > float(topk)) & jnp.logical_not(few), 1.0, 0.0)

    @pl.when(jnp.max(surplus) > 0.0)
    def _():
        nbits = (seq - 1).bit_length()

        def idx_step(bi, ans):
            cand = ans + lax.shift_left(jnp.int32(1), nbits - 1 - bi)
            return jnp.where(count(lambda sv, kp: (sv == thr) & (kp < cand)) < need, cand, ans)

        jcut_ref[...] = jnp.broadcast_to(lax.fori_loop(0, nbits, idx_step, jnp.zeros((1, tq), I32)), jcut_ref.shape)

    thr_sel = jnp.where(few, -jnp.inf, thr)
    jcut = jnp.where(few, -1, jcut_ref[0:1, :])

    def bias_chunk(c, carry):
        c0 = pl.multiple_of(c * sc, sc)
        sv = s_ref[pl.ds(c0, sc), :]
        sel = (sv > thr_sel) | ((sv == thr_sel) & (c0 + kpos <= jcut))
        for r0 in range(0, tq, sc):
            bias_ref[r0:r0 + sc, pl.ds(c0, sc)] = jnp.where(sel, 0.0, NEG)[:, r0:r0 + sc].T
        return carry

    lax.fori_loop(0, n_kc * (ck // sc), bias_chunk, 0)

    q_scale = B_HEAD_DIM ** -0.5 * LOG2E
    for p in range(B_HEADS // 2):
        ls = slice(p * LANES, (p + 1) * LANES)
        q2 = q_ref[:, ls].astype(F32) * q_scale
        qe_ref[:, ls] = jnp.where(lane < B_HEAD_DIM, q2, 0.0).astype(BF16)
        qo_ref[:, ls] = jnp.where(lane >= B_HEAD_DIM, q2, 0.0).astype(BF16)
    bias_fn = lambda c0: bias_ref[:, pl.ds(c0, ck)]
    group = 2
    for p0 in range(0, B_HEADS // 2, group):
        pairs = [(qe_ref, p * LANES, qo_ref, p * LANES, k_ref, p * LANES, p * LANES, v_ref, p * LANES)
                 for p in range(p0, p0 + group)]
        for p, o in zip(range(p0, p0 + group), _flash_pairs(pairs, tq, 0, n_kc, ck, bias_fn)):
            o_ref[:, p * LANES:(p + 1) * LANES] = o.astype(BF16)


def _dsa_mixer(x, w_qkv, w_idx, w_out, g, b, *, batch, seq, tq=256, ck=512):
    m = x.shape[0]
    nq = seq // tq
    hd = B_HEADS * B_HEAD_DIM
    topk = min(B_TOPK_MAX, seq // 4)
    qkv = _matmul(x, w_qkv, tm=512, tn=1024, out_dtype=BF16)
    idx = _matmul(x, w_idx, tm=512, tn=512, out_dtype=F32, precision=HIGHEST)
    w_rows = jnp.pad(idx[:, 3 * LANES:3 * LANES + B_IDX_HEADS].T, ((0, 8 - B_IDX_HEADS), (0, 0)))
    attn = pl.pallas_call(
        functools.partial(_dsa_body, tq=tq, ck=ck, topk=topk),
        name="dsa_attn",
        grid=(batch, nq),
        in_specs=[pl.BlockSpec((tq, hd), lambda bi, i: (bi * nq + i, 0)),
                  pl.BlockSpec((seq, hd), lambda bi, i: (bi, 1)),
                  pl.BlockSpec((seq, hd), lambda bi, i: (bi, 2)),
                  pl.BlockSpec((tq, 2 * LANES), lambda bi, i: (bi * nq + i, 0)),
                  pl.BlockSpec((seq, LANES), lambda bi, i: (bi, 2)),
                  pl.BlockSpec((8, tq), lambda bi, i: (0, bi * nq + i))],
        out_specs=pl.BlockSpec((tq, hd), lambda bi, i: (bi * nq + i, 0)),
        out_shape=jax.ShapeDtypeStruct((m, hd), BF16),
        scratch_shapes=[pltpu.VMEM((seq, tq), F32), pltpu.VMEM((tq, seq), F32), pltpu.VMEM((tq, hd), BF16),
                        pltpu.VMEM((tq, hd), BF16), pltpu.VMEM((tq, B_IDX_HEADS * LANES), BF16),
                        pltpu.VMEM((tq, B_IDX_HEADS * LANES), BF16), pltpu.VMEM((8, tq), I32)],
        compiler_params=_params("parallel", "arbitrary"),
    )(qkv, qkv, qkv, idx, idx, w_rows)
    return _proj_ln(attn, w_out, x, g, b)


def _hgrn2_body(q_ref, f_ref, i_ref, gt_ref, lbl_ref, ng_ref, o_ref, st_ref, *, layer):
    @pl.when(pl.program_id(2) == 0)
    def _():
        st_ref[...] = jnp.zeros_like(st_ref)

    lg = lbl_ref[...]
    e = jnp.exp(lg - jnp.max(lg, axis=0, keepdims=True))
    sm = e / jnp.sum(e, axis=0, keepdims=True)
    lb_all = [sm[0:1]]
    for d in range(1, DEPTH):
        lb_all.append(lb_all[-1] + sm[d:d + 1])
    lb = lb_all[layer] - lb_all[0]
    ng = ng_ref[...]
    rid = lax.broadcasted_iota(I32, (C_SUB, LANES), 0)
    rid8 = lax.broadcasted_iota(I32, (C_SUB // 2, LANES), 0)

    def sub_chunk(j, carry):
        r0 = pl.multiple_of(j * C_SUB, C_SUB)
        for h in range(C_HG):
            ls = slice(h * LANES, (h + 1) * LANES)
            qb = q_ref[pl.ds(r0, C_SUB), ls]
            lbh = lb[:, ls]
            f = lbh + (1.0 - lbh) * jax.nn.sigmoid(f_ref[pl.ds(r0, C_SUB), ls])
            gb = jnp.log(f)
            kb = 1.0 - f
            vb = i_ref[pl.ds(r0, C_SUB), ls]
            gc = gb
            for sh in (1, 2, 4, 8):
                gc = gc + jnp.where(rid >= sh, pltpu.roll(gc, sh, 0), 0.0)
            glast = gc[C_SUB - 1:C_SUB, :]
            st = st_ref[h]
            out = _nt_dot((qb * jnp.exp(gc)).astype(BF16), st.astype(BF16))
            half = C_SUB // 2
            intra = []
            for blk in range(2):
                rows = slice(blk * half, (blk + 1) * half)
                q_b, g_b = qb[rows], gc[rows]
                acc = jnp.zeros((half, LANES), F32)
                for s in range((blk + 1) * half):
                    w = q_b * kb[s:s + 1, :]
                    if s >= blk * half:
                        keep = rid8 >= s - blk * half
                        w = jnp.where(keep, w * jnp.exp(jnp.where(keep, g_b - gc[s:s + 1, :], 0.0)), 0.0)
                    else:
                        w = w * jnp.exp(g_b - gc[s:s + 1, :])
                    acc = acc + jnp.sum(w, axis=1, keepdims=True) * vb[s:s + 1, :]
                intra.append(acc)
            out = out + jnp.concatenate(intra, axis=0)
            kdec = kb * jnp.exp(glast - gc)
            kv_t = lax.dot_general(vb.astype(BF16), kdec.astype(BF16), (((0,), (0,)), ((), ())),
                                   preferred_element_type=F32)
            st_ref[h] = st * jnp.exp(glast) + kv_t
            gate = gt_ref[pl.ds(r0, C_SUB), ls]
            o = out * lax.rsqrt(jnp.mean(out * out, axis=1, keepdims=True) + RMS_EPS) * ng
            o_ref[pl.ds(r0, C_SUB), ls] = (o * (gate * jax.nn.sigmoid(gate))).astype(BF16)
        return carry

    lax.fori_loop(0, q_ref.shape[0] // C_SUB, sub_chunk, 0)


def _hgrn2_mixer(x, w_in, lb_logits, norm_g, w_out, g, b, *, batch, seq, layer):
    m = x.shape[0]
    ts = min(C_TS, seq)
    ns = seq // ts
    w = C_HG * LANES
    nhb = C_HEADS // C_HG
    proj = _matmul(x, w_in, tm=512, tn=1024, out_dtype=F32)
    spec = lambda blk: pl.BlockSpec((ts, w), lambda bi, hb, s: (bi * ns + s, blk * nhb + hb))
    o = pl.pallas_call(
        functools.partial(_hgrn2_body, layer=layer),
        name="hgrn2_scan",
        grid=(batch, nhb, ns),
        in_specs=[spec(0), spec(1), spec(2), spec(3),
                  pl.BlockSpec((DEPTH, w), lambda bi, hb, s: (0, hb)),
                  pl.BlockSpec((1, LANES), lambda bi, hb, s: (0, 0))],
        out_specs=pl.BlockSpec((ts, w), lambda bi, hb, s: (bi * ns + s, hb)),
        out_shape=jax.ShapeDtypeStruct((m, D_MODEL), BF16),
        scratch_shapes=[pltpu.VMEM((C_HG, C_HEAD_V, C_EXPAND), F32)],
        compiler_params=_params("parallel", "parallel", "arbitrary"),
    )(proj, proj, proj, proj, lb_logits, norm_g)
    return _proj_ln(o, w_out, x, g, b)


def _mla_prep_body(p_ref, pos_ref, invf_ref, qg_ref, kvg_ref, wq_ref, wk_ref, wv_ref, q_ref, k_ref, v_ref):
    tm = p_ref.shape[0]
    lane = lax.broadcasted_iota(I32, (tm, LANES), 1)
    ang = pos_ref[...].astype(F32) * invf_ref[...]
    cos_t = jnp.cos(ang)
    sin_t = jnp.sin(ang)
    cq = p_ref[:, 0:D_Q_RANK]
    cq = cq * lax.rsqrt(jnp.mean(cq * cq, axis=1, keepdims=True) + RMS_EPS) * qg_ref[...]
    ckv = p_ref[:, D_Q_RANK:D_Q_RANK + D_KV_RANK]
    ckv = ckv * lax.rsqrt(jnp.mean(ckv * ckv, axis=1, keepdims=True) + RMS_EPS) * kvg_ref[...]
    k_rot = p_ref[:, 3 * LANES:4 * LANES] * cos_t + p_ref[:, 4 * LANES:5 * LANES] * sin_t
    q_tab = jnp.where(lane < D_NOPE + D_ROPE, cos_t, sin_t) * ((D_NOPE + D_ROPE) ** -0.5 * LOG2E)
    cqb = cq.astype(BF16)
    ckvb = ckv.astype(BF16)
    for h in range(D_HEADS):
        ls = slice(h * LANES, (h + 1) * LANES)
        qh = jnp.dot(cqb, wq_ref[:, ls], preferred_element_type=F32)
        q_ref[:, ls] = (qh * q_tab).astype(BF16)
        kh = jnp.dot(ckvb, wk_ref[:, ls], preferred_element_type=F32)
        k_ref[:, ls] = (kh + k_rot).astype(BF16)
    v_ref[...] = jnp.dot(ckvb, wv_ref[...], preferred_element_type=F32).astype(BF16)


def _mla_attn_body(q_ref, k_ref, v_ref, o_ref, *, tq, ck):
    i = pl.program_id(2)
    n_kc = (i * tq + tq + ck - 1) // ck
    n_plain = (i * tq + 1) // ck
    qpos = i * tq + lax.broadcasted_iota(I32, (tq, ck), 0)
    kpos = lax.broadcasted_iota(I32, (tq, ck), 1)
    bias_fn = lambda c0: jnp.where(c0 + kpos <= qpos, 0.0, NEG)
    pairs = [(q_ref, 2 * p * LANES, q_ref, (2 * p + 1) * LANES, k_ref, 2 * p * LANES, (2 * p + 1) * LANES, v_ref, p * LANES)
             for p in range(D_PAIRS)]
    for p, o in enumerate(_flash_pairs(pairs, tq, n_plain, n_kc, ck, bias_fn)):
        o_ref[:, p * LANES:(p + 1) * LANES] = o.astype(BF16)


def _mla_mixer(x, positions, w_in, invf, q_norm_g, kv_norm_g, w_q, w_k, w_v, w_out, g, b, *, batch, seq, tm=256, tq=256, ck=512):
    m = x.shape[0]
    nq = seq // tq
    hw = D_HEADS * LANES
    proj = _matmul(x, w_in, tm=512, tn=w_in.shape[1], out_dtype=F32)
    hv = D_HEADS * D_VDIM
    q, k, v = pl.pallas_call(
        _mla_prep_body,
        name="mla_prep",
        grid=(m // tm,),
        in_specs=[pl.BlockSpec((tm, w_in.shape[1]), lambda i: (i, 0)), pl.BlockSpec((tm, 1), lambda i: (i, 0)),
                  pl.BlockSpec((1, LANES), lambda i: (0, 0)), pl.BlockSpec((1, D_Q_RANK), lambda i: (0, 0)),
                  pl.BlockSpec((1, D_KV_RANK), lambda i: (0, 0)), pl.BlockSpec((D_Q_RANK, hw), lambda i: (0, 0)),
                  pl.BlockSpec((D_KV_RANK, hw), lambda i: (0, 0)),
                  pl.BlockSpec((D_KV_RANK, hv), lambda i: (0, 0))],
        out_specs=[pl.BlockSpec((tm, hw), lambda i: (i, 0)), pl.BlockSpec((tm, hw), lambda i: (i, 0)),
                   pl.BlockSpec((tm, hv), lambda i: (i, 0))],
        out_shape=[jax.ShapeDtypeStruct((m, hw), BF16), jax.ShapeDtypeStruct((m, hw), BF16),
                   jax.ShapeDtypeStruct((m, hv), BF16)],
        compiler_params=_params("parallel"),
    )(proj, positions.reshape(m, 1), invf, q_norm_g, kv_norm_g, w_q, w_k, w_v)
    attn = pl.pallas_call(
        functools.partial(_mla_attn_body, tq=tq, ck=ck),
        name="mla_attn",
        grid=(batch, D_HEADS // (2 * D_PAIRS), nq),
        in_specs=[pl.BlockSpec((tq, 2 * D_PAIRS * LANES), lambda bi, hp, i: (bi * nq + i, hp)),
                  pl.BlockSpec((seq, 2 * D_PAIRS * LANES), lambda bi, hp, i: (bi, hp)),
                  pl.BlockSpec((seq, D_PAIRS * LANES), lambda bi, hp, i: (bi, hp))],
        out_specs=pl.BlockSpec((tq, D_PAIRS * LANES), lambda bi, hp, i: (bi * nq + i, hp)),
        out_shape=jax.ShapeDtypeStruct((m, hv), BF16),
        compiler_params=_params("parallel", "parallel", "arbitrary"),
    )(q, k, v)
    return _proj_ln(attn, w_out, x, g, b)


def _rotate_half_cols(w):
    half = w.shape[-1] // 2
    return jnp.concatenate([-w[..., half:], w[..., :half]], axis=-1)


def _mla_weights(d_w_in, d_w_uq, d_w_ukv):
    zeros = lambda *s: jnp.zeros(s, F32)
    w_kr = d_w_in[:, D_Q_RANK + D_KV_RANK:]
    w_krh = _rotate_half_cols(w_kr)
    z64 = zeros(D_MODEL, D_NOPE)
    w_in = jnp.concatenate([d_w_in[:, :D_Q_RANK + D_KV_RANK], z64, w_kr, w_kr, z64, w_krh, w_krh], axis=1)
    uq = d_w_uq.reshape(D_Q_RANK, D_HEADS, D_NOPE + D_ROPE)
    w_q = jnp.concatenate([uq, _rotate_half_cols(uq[..., D_NOPE:])], axis=-1).reshape(D_Q_RANK, D_HEADS * LANES)
    ukv = d_w_ukv.reshape(D_KV_RANK, D_HEADS, D_NOPE + D_VDIM)
    w_k = jnp.concatenate([ukv[..., :D_NOPE], zeros(D_KV_RANK, D_HEADS, LANES - D_NOPE)], axis=-1)
    w_k = w_k.reshape(D_KV_RANK, D_HEADS * LANES)
    w_v = ukv[..., D_NOPE:].reshape(D_KV_RANK, D_HEADS * D_VDIM)
    half = D_ROPE // 2
    inv_freq = ROPE_BASE ** (-jnp.arange(half, dtype=F32) / half)
    invf = jnp.concatenate([jnp.zeros((D_NOPE,), F32), jnp.tile(inv_freq, 4)]).reshape(1, LANES)
    return w_in.astype(BF16), w_q.astype(BF16), w_k.astype(BF16), w_v.astype(BF16), invf


def _dsa_weights(b_w_in):
    hd = B_HEADS * B_HEAD_DIM
    o = 3 * hd
    nqi = B_IDX_HEADS * B_IDX_DIM
    w_qi = b_w_in[:, o:o + nqi]
    w_ki = b_w_in[:, o + nqi:o + nqi + B_IDX_DIM]
    w_wi = b_w_in[:, o + nqi + B_IDX_DIM:]
    pad = jnp.zeros((D_MODEL, LANES - B_IDX_HEADS), F32)
    w_idx = jnp.concatenate([w_qi, w_ki, w_ki, w_wi, pad], axis=1)
    return b_w_in[:, :o].astype(BF16), w_idx


def kernel(x, positions, a_w_in, a_ln_g, a_ln_b, a_w_s, a_b_s, a_w_out, b_w_in, b_w_out, c_w_in, c_lb_logits, c_norm_g, c_w_out, d_w_in, d_q_norm_g, d_w_uq, d_kv_norm_g, d_w_ukv, d_w_out, ffn0_w_gu, ffn0_w_down, moe1_w_router, moe1_w_gu, moe1_w_down, ffn2_w_gu, ffn2_w_down, moe3_w_router, moe3_w_gu, moe3_w_down, ln_mix_g, ln_mix_b, ln_ffn_g, ln_ffn_b):
    batch, seq, d = x.shape
    m = batch * seq
    row = lambda v: v.reshape(1, -1).astype(F32)
    bf = lambda w: w.astype(BF16)
    h = x.reshape(m, d)

    h = _gmlp_mixer(h, bf(a_w_in), row(a_ln_g), row(a_ln_b), a_w_s, a_b_s.T, bf(a_w_out),
                    row(ln_mix_g[0]), row(ln_mix_b[0]))
    h = _dense_ffn(h, ffn0_w_gu, ffn0_w_down, row(ln_ffn_g[0]), row(ln_ffn_b[0]))

    w_qkv, w_idx = _dsa_weights(b_w_in)
    h = _dsa_mixer(h, w_qkv, w_idx, bf(b_w_out), row(ln_mix_g[1]), row(ln_mix_b[1]), batch=batch, seq=seq)
    h = _moe_ffn(h, moe1_w_router.T, moe1_w_gu, moe1_w_down, row(ln_ffn_g[1]), row(ln_ffn_b[1]))

    h = _hgrn2_mixer(h, bf(c_w_in), c_lb_logits, row(c_norm_g), bf(c_w_out), row(ln_mix_g[2]), row(ln_mix_b[2]),
                     batch=batch, seq=seq, layer=2)
    h = _dense_ffn(h, ffn2_w_gu, ffn2_w_down, row(ln_ffn_g[2]), row(ln_ffn_b[2]))

    w_in, w_q, w_k, w_v, invf = _mla_weights(d_w_in, d_w_uq, d_w_ukv)
    h = _mla_mixer(h, positions, w_in, invf, row(d_q_norm_g), row(d_kv_norm_g), w_q, w_k, w_v, bf(d_w_out),
                   row(ln_mix_g[3]), row(ln_mix_b[3]), batch=batch, seq=seq)
    h = _moe_ffn(h, moe3_w_router.T, moe3_w_gu, moe3_w_down, row(ln_ffn_g[3]), row(ln_ffn_b[3]))
    return h.reshape(batch, seq, d)
```

```python
import functools

import jax
import jax.numpy as jnp
from jax import lax
from jax.experimental import pallas as pl
from jax.experimental.pallas import tpu as pltpu

F32 = jnp.float32
BF16 = jnp.bfloat16
I32 = jnp.int32

D_MODEL = 1024
DEPTH = 4
ALPHA = (2.0 * DEPTH) ** 0.25
LN_EPS = 1e-5
RMS_EPS = 1e-6

A_CHUNK = 128
A_HALF = 2 * D_MODEL
A_GROUPS = 8
A_GROUP_DIM = A_HALF // A_GROUPS

B_HEADS = 16
B_HEAD_DIM = 64
B_IDX_HEADS = 4
B_IDX_DIM = 64
B_TOPK_MAX = 256

C_HEADS = 8
C_EXPAND = 128
C_HEAD_V = 128
C_FDIM = C_HEADS * C_EXPAND
C_SUB = 16
C_HG = 4
C_TS = 512

D_HEADS = 16
D_NOPE = 64
D_ROPE = 32
D_VDIM = 64
D_Q_RANK = 256
D_KV_RANK = 128
D_PAIRS = 4
ROPE_BASE = 10000.0

FFN_DIM = 3584
N_EXPERTS = 8
TOP_K = 2

LANES = 128
VMEM_LIMIT = 56 * 1024 * 1024
NEG = -1e30
LOG2E = 1.4426950408889634
INT_MIN = -2147483648
HIGHEST = lax.Precision.HIGHEST


def _params(*sem):
    return pltpu.CompilerParams(dimension_semantics=sem, vmem_limit_bytes=VMEM_LIMIT)


def _layer_norm(y, g, b):
    mu = jnp.mean(y, axis=-1, keepdims=True)
    d = y - mu
    var = jnp.mean(d * d, axis=-1, keepdims=True)
    return d * lax.rsqrt(var + LN_EPS) * g + b


def _nt_dot(a, b, precision=None):
    return lax.dot_general(a, b, (((1,), (1,)), ((), ())), preferred_element_type=F32, precision=precision)


def _mm_body(a_ref, w_ref, o_ref, *, act, precision):
    a = a_ref[...]
    if precision is None:
        a = a.astype(BF16)
    acc = jnp.dot(a, w_ref[...], preferred_element_type=F32, precision=precision)
    if act == "gelu":
        acc = jax.nn.gelu(acc)
    o_ref[...] = acc.astype(o_ref.dtype)


def _matmul(a, w, *, tm, tn, out_dtype, act=None, precision=None):
    m, k = a.shape
    n = w.shape[1]
    return pl.pallas_call(
        functools.partial(_mm_body, act=act, precision=precision),
        name="matmul",
        grid=(m // tm, n // tn),
        in_specs=[pl.BlockSpec((tm, k), lambda i, j: (i, 0)), pl.BlockSpec((k, tn), lambda i, j: (0, j))],
        out_specs=pl.BlockSpec((tm, tn), lambda i, j: (i, j)),
        out_shape=jax.ShapeDtypeStruct((m, n), out_dtype),
        compiler_params=_params("parallel", "parallel"),
    )(a, w)


def _proj_ln_body(a_ref, w_ref, x_ref, g_ref, b_ref, o_ref):
    h = jnp.dot(a_ref[...].astype(BF16), w_ref[...], preferred_element_type=F32)
    o_ref[...] = _layer_norm(ALPHA * x_ref[...] + h, g_ref[...], b_ref[...])


def _proj_ln(a, w, x, g, b, *, tm=512):
    m, k = a.shape
    return pl.pallas_call(
        _proj_ln_body,
        name="proj_ln",
        grid=(m // tm,),
        in_specs=[pl.BlockSpec((tm, k), lambda i: (i, 0)), pl.BlockSpec((k, D_MODEL), lambda i: (0, 0)),
                  pl.BlockSpec((tm, D_MODEL), lambda i: (i, 0)), pl.BlockSpec((1, D_MODEL), lambda i: (0, 0)),
                  pl.BlockSpec((1, D_MODEL), lambda i: (0, 0))],
        out_specs=pl.BlockSpec((tm, D_MODEL), lambda i: (i, 0)),
        out_shape=jax.ShapeDtypeStruct((m, D_MODEL), F32),
        compiler_params=_params("parallel"),
    )(a, w, x, g, b)


def _ffn_body(x_ref, wg_ref, wu_ref, wd_ref, g_ref, b_ref, o_ref, acc_ref, xb_ref):
    j = pl.program_id(1)

    @pl.when(j == 0)
    def _():
        acc_ref[...] = jnp.zeros_like(acc_ref)
        xb_ref[...] = x_ref[...].astype(BF16)

    xb = xb_ref[...]
    gate = jnp.dot(xb, wg_ref[...].astype(BF16), preferred_element_type=F32)
    up = jnp.dot(xb, wu_ref[...].astype(BF16), preferred_element_type=F32)
    h = (gate * jax.nn.sigmoid(gate)) * up
    acc_ref[...] += jnp.dot(h.astype(BF16), wd_ref[...].astype(BF16), preferred_element_type=F32)

    @pl.when(j == pl.num_programs(1) - 1)
    def _():
        o_ref[...] = _layer_norm(ALPHA * x_ref[...] + acc_ref[...], g_ref[...], b_ref[...])


def _dense_ffn(x, w_gu, w_down, g, b, *, tm=1024, tf=512):
    m = x.shape[0]
    nf = FFN_DIM // tf
    return pl.pallas_call(
        _ffn_body,
        name="dense_ffn",
        grid=(m // tm, nf),
        in_specs=[pl.BlockSpec((tm, D_MODEL), lambda i, j: (i, 0)),
                  pl.BlockSpec((D_MODEL, tf), lambda i, j: (0, j)),
                  pl.BlockSpec((D_MODEL, tf), lambda i, j: (0, nf + j)),
                  pl.BlockSpec((tf, D_MODEL), lambda i, j: (j, 0)),
                  pl.BlockSpec((1, D_MODEL), lambda i, j: (0, 0)),
                  pl.BlockSpec((1, D_MODEL), lambda i, j: (0, 0))],
        out_specs=pl.BlockSpec((tm, D_MODEL), lambda i, j: (i, 0)),
        out_shape=jax.ShapeDtypeStruct((m, D_MODEL), F32),
        scratch_shapes=[pltpu.VMEM((tm, D_MODEL), F32), pltpu.VMEM((tm, D_MODEL), BF16)],
        compiler_params=_params("parallel", "arbitrary"),
    )(x, w_gu, w_gu, w_down, g, b)


ROW_TILE = (8, LANES)


def _to_row_tiles(dst_ref, val):
    for k in range(ROW_TILE[0]):
        dst_ref[:, k, :] = val[:, k * LANES:(k + 1) * LANES]


def _from_row_tiles(src_ref, r0, n):
    return jnp.concatenate([src_ref[r0:r0 + n, k, :] for k in range(ROW_TILE[0])], axis=1)


def _router_body(x_ref, wr_ref, e_ref, g_ref):
    logits = _nt_dot(wr_ref[...], x_ref[...], precision=HIGHEST)
    eid = lax.broadcasted_iota(I32, logits.shape, 0).astype(F32)
    m1 = jnp.max(logits, axis=0, keepdims=True)
    i1 = jnp.min(jnp.where(logits == m1, eid, float(N_EXPERTS)), axis=0, keepdims=True)
    rest = jnp.where(eid == i1, -jnp.inf, logits)
    m2 = jnp.max(rest, axis=0, keepdims=True)
    i2 = jnp.min(jnp.where(rest == m2, eid, float(N_EXPERTS)), axis=0, keepdims=True)
    e2 = jnp.exp(m2 - m1)
    den = 1.0 + e2
    e_ref[...] = jnp.concatenate([i1, i2], axis=0).astype(I32)
    g_ref[...] = jnp.concatenate([1.0 / den, e2 / den], axis=0)


def _router(x, w_router_t, *, tm=512):
    m = x.shape[0]
    return pl.pallas_call(
        _router_body,
        name="router",
        grid=(m // tm,),
        in_specs=[pl.BlockSpec((tm, D_MODEL), lambda i: (i, 0)), pl.BlockSpec((N_EXPERTS, D_MODEL), lambda i: (0, 0))],
        out_specs=[pl.BlockSpec((TOP_K, tm), lambda i: (0, i)), pl.BlockSpec((TOP_K, tm), lambda i: (0, i))],
        out_shape=[jax.ShapeDtypeStruct((TOP_K, m), I32), jax.ShapeDtypeStruct((TOP_K, m), F32)],
        compiler_params=_params("parallel"),
    )(x, w_router_t)


def _start_row_gather(idx_ref, src_hbm, dst_ref, sem):
    def start(r, c):
        pltpu.make_async_copy(src_hbm.at[pl.ds(idx_ref[0, 0, r], 1)], dst_ref.at[pl.ds(r, 1)], sem).start()
        return c

    lax.fori_loop(0, dst_ref.shape[0], start, 0, unroll=8)


def _wait_row_gather(src_hbm, dst_ref, sem):
    pltpu.make_async_copy(src_hbm.at[pl.ds(0, dst_ref.shape[0])], dst_ref, sem).wait()


def _moe_body(te_ref, nu_ref, tok_ref, nxt_ref, x_hbm, wg_ref, wu_ref, wd_ref, y_ref, xrow_ref, xb_ref, acc_ref, sem):
    del te_ref
    i = pl.program_id(0)
    j = pl.program_id(1)
    tm = xb_ref.shape[0]
    n_used = nu_ref[0]

    @pl.when((j == 0) & (i < n_used))
    def _():
        slot = lax.rem(i, 2)

        @pl.when(i == 0)
        def _():
            _start_row_gather(tok_ref, x_hbm, xrow_ref.at[0], sem.at[0])

        @pl.when(i + 1 < n_used)
        def _():
            _start_row_gather(nxt_ref, x_hbm, xrow_ref.at[1 - slot], sem.at[1 - slot])

        _wait_row_gather(x_hbm, xrow_ref.at[slot], sem.at[slot])
        xb_ref[...] = xrow_ref[slot].astype(BF16)
        acc_ref[...] = jnp.zeros_like(acc_ref)

    @pl.when(i < n_used)
    def _():
        xb = xb_ref[...]
        gate = jnp.dot(xb, wg_ref[...].astype(BF16), preferred_element_type=F32)
        up = jnp.dot(xb, wu_ref[...].astype(BF16), preferred_element_type=F32)
        h = (gate * jax.nn.sigmoid(gate)) * up
        acc_ref[...] += jnp.dot(h.astype(BF16), wd_ref[...].astype(BF16), preferred_element_type=F32)

    @pl.when(j == pl.num_programs(1) - 1)
    def _():
        @pl.when(i < n_used)
        def _():
            _to_row_tiles(y_ref, acc_ref[...])

        @pl.when(i >= n_used)
        def _():
            y_ref[...] = jnp.zeros_like(y_ref)


def _combine_body(pos_ref, nxt_ref, x_ref, gt_ref, y_hbm, g_ref, b_ref, o_ref, rows_ref, sem):
    i = pl.program_id(0)
    tq = x_ref.shape[0]
    slot = lax.rem(i, 2)

    @pl.when(i == 0)
    def _():
        _start_row_gather(pos_ref, y_hbm, rows_ref.at[0], sem.at[0])

    @pl.when(i + 1 < pl.num_programs(0))
    def _():
        _start_row_gather(nxt_ref, y_hbm, rows_ref.at[1 - slot], sem.at[1 - slot])

    _wait_row_gather(y_hbm, rows_ref.at[slot], sem.at[slot])
    rows = rows_ref.at[slot]
    gt = gt_ref[...]
    h = gt[:, 0:1] * _from_row_tiles(rows, 0, tq) + gt[:, 1:2] * _from_row_tiles(rows, tq, tq)
    o_ref[...] = _layer_norm(ALPHA * x_ref[...] + h, g_ref[...], b_ref[...])


def _moe_ffn(x, w_router_t, w_gu, w_down, g, b, *, tm=1024, tf=512, tq=256):
    m = x.shape[0]
    n_assign = m * TOP_K
    n_tiles = n_assign // tm + N_EXPERTS
    n_rows = n_tiles * tm
    nf = FFN_DIM // tf

    eidx, gate = _router(x, w_router_t)
    flat_e = eidx.reshape(-1)
    onehot = (flat_e[:, None] == jnp.arange(N_EXPERTS, dtype=I32)[None, :]).astype(I32)
    csum = jnp.cumsum(onehot, axis=0)
    rank = jnp.sum(onehot * csum, axis=1) - 1
    counts = csum[-1]
    padded = (counts + tm - 1) // tm * tm
    padded_ends = jnp.cumsum(padded)
    padded_starts = padded_ends - padded
    dest = padded_starts[flat_e] + rank
    tok = jnp.tile(jnp.arange(m, dtype=I32), TOP_K)
    row_tok3 = jnp.zeros((n_rows,), I32).at[dest].set(tok).reshape(n_tiles, 1, tm)
    n_used = (padded_ends[-1:] // tm).astype(I32)
    w_step = lambda i, j, nu: jnp.where(i < nu[0], j, nf - 1)
    tile_start = jnp.arange(n_tiles, dtype=I32) * tm
    tile_e = jnp.minimum(jnp.sum((tile_start[:, None] >= padded_ends[None, :]).astype(I32), axis=1), N_EXPERTS - 1)

    y = pl.pallas_call(
        _moe_body,
        name="moe_ffn",
        grid_spec=pltpu.PrefetchScalarGridSpec(
            num_scalar_prefetch=2,
            grid=(n_tiles, nf),
            in_specs=[pl.BlockSpec((1, 1, tm), lambda i, j, te, nu: (i, 0, 0), memory_space=pltpu.SMEM),
                      pl.BlockSpec((1, 1, tm), lambda i, j, te, nu: (jnp.minimum(i + 1, n_tiles - 1), 0, 0),
                                   memory_space=pltpu.SMEM),
                      pl.BlockSpec(memory_space=pl.ANY),
                      pl.BlockSpec((None, D_MODEL, tf), lambda i, j, te, nu: (te[i], 0, w_step(i, j, nu))),
                      pl.BlockSpec((None, D_MODEL, tf), lambda i, j, te, nu: (te[i], 0, nf + w_step(i, j, nu))),
                      pl.BlockSpec((None, tf, D_MODEL), lambda i, j, te, nu: (te[i], w_step(i, j, nu), 0))],
            out_specs=pl.BlockSpec((tm,) + ROW_TILE, lambda i, j, te, nu: (i, 0, 0)),
            scratch_shapes=[pltpu.VMEM((2, tm, D_MODEL), F32), pltpu.VMEM((tm, D_MODEL), BF16),
                            pltpu.VMEM((tm, D_MODEL), F32), pltpu.SemaphoreType.DMA((2,))]),
        out_shape=jax.ShapeDtypeStruct((n_rows,) + ROW_TILE, F32),
        compiler_params=_params("arbitrary", "arbitrary"),
    )(tile_e, n_used, row_tok3, row_tok3, x, w_gu, w_gu, w_down)

    pos = jnp.concatenate([dest[:m].reshape(m // tq, 1, tq), dest[m:].reshape(m // tq, 1, tq)], axis=2)
    return pl.pallas_call(
        _combine_body,
        name="moe_combine",
        grid=(m // tq,),
        in_specs=[pl.BlockSpec((1, 1, 2 * tq), lambda i: (i, 0, 0), memory_space=pltpu.SMEM),
                  pl.BlockSpec((1, 1, 2 * tq), lambda i: (jnp.minimum(i + 1, m // tq - 1), 0, 0),
                               memory_space=pltpu.SMEM),
                  pl.BlockSpec((tq, D_MODEL), lambda i: (i, 0)),
                  pl.BlockSpec((tq, TOP_K), lambda i: (i, 0)),
                  pl.BlockSpec(memory_space=pl.ANY),
                  pl.BlockSpec((1, D_MODEL), lambda i: (0, 0)),
                  pl.BlockSpec((1, D_MODEL), lambda i: (0, 0))],
        out_specs=pl.BlockSpec((tq, D_MODEL), lambda i: (i, 0)),
        out_shape=jax.ShapeDtypeStruct((m, D_MODEL), F32),
        scratch_shapes=[pltpu.VMEM((2, 2 * tq) + ROW_TILE, F32), pltpu.SemaphoreType.DMA((2,))],
        compiler_params=_params("arbitrary"),
    )(pos, pos, x, gate.T, y, g, b)


def _gmlp_gate_body(u_ref, v_ref, lg_ref, lb_ref, ws_ref, bs_ref, o_ref):
    tm = u_ref.shape[0]
    v = _layer_norm(v_ref[...].astype(F32), lg_ref[...], lb_ref[...]).astype(BF16)
    r = lax.broadcasted_iota(I32, (A_CHUNK, A_CHUNK), 0)
    c = lax.broadcasted_iota(I32, (A_CHUNK, A_CHUNK), 1)
    causal = c <= r
    bs = bs_ref[...]
    for grp in range(A_GROUPS):
        wc = jnp.where(causal, ws_ref[grp], 0.0).astype(BF16)
        lo = grp * A_GROUP_DIM
        for ch in range(tm // A_CHUNK):
            r0 = ch * A_CHUNK
            s = jnp.dot(wc, v[r0:r0 + A_CHUNK, lo:lo + A_GROUP_DIM], preferred_element_type=F32)
            s = s + bs[:, grp:grp + 1]
            u = u_ref[r0:r0 + A_CHUNK, lo:lo + A_GROUP_DIM].astype(F32)
            o_ref[r0:r0 + A_CHUNK, lo:lo + A_GROUP_DIM] = (u * s).astype(BF16)


def _gmlp_mixer(x, w_in, ln_g, ln_b, w_s, b_s_t, w_out, g, b, *, tm=256):
    m = x.shape[0]
    uv = _matmul(x, w_in, tm=512, tn=1024, out_dtype=BF16, act="gelu")
    gated = pl.pallas_call(
        _gmlp_gate_body,
        name="gmlp_gate",
        grid=(m // tm,),
        in_specs=[pl.BlockSpec((tm, A_HALF), lambda i: (i, 0)), pl.BlockSpec((tm, A_HALF), lambda i: (i, 1)),
                  pl.BlockSpec((1, A_HALF), lambda i: (0, 0)), pl.BlockSpec((1, A_HALF), lambda i: (0, 0)),
                  pl.BlockSpec((A_GROUPS, A_CHUNK, A_CHUNK), lambda i: (0, 0, 0)),
                  pl.BlockSpec((A_CHUNK, A_GROUPS), lambda i: (0, 0))],
        out_specs=pl.BlockSpec((tm, A_HALF), lambda i: (i, 0)),
        out_shape=jax.ShapeDtypeStruct((m, A_HALF), BF16),
        compiler_params=_params("parallel"),
    )(uv, uv, ln_g, ln_b, w_s, b_s_t)
    return _proj_ln(gated, w_out, x, g, b)


def _flash_pairs(pairs, tq, n_plain, n_kc, ck, bias_fn):
    even = lax.broadcasted_iota(I32, (tq, LANES), 1) < (LANES // 2)
    ones = jnp.ones((ck, LANES), BF16)

    def head(q_ref, q_lane, k_ref, k_lane, c0, bias, v_aug, m):
        s = _nt_dot(q_ref[:, q_lane:q_lane + LANES], k_ref[pl.ds(c0, ck), k_lane:k_lane + LANES])
        if bias is not None:
            s = s + bias
        m_new = jnp.maximum(m, jnp.max(s, axis=1, keepdims=True))
        p = jnp.exp2(s - m_new).astype(BF16)
        pv = jnp.dot(p, v_aug, preferred_element_type=F32)
        return m_new, jnp.exp2(m - m_new), pv

    def step(c, carry, masked):
        c0 = pl.multiple_of(c * ck, ck)
        bias = bias_fn(c0) if masked else None
        new = []
        for (qe_ref, qe_lane, qo_ref, qo_lane, k_ref, kle, klo, v_ref, v_lane), (me, mo, l, acc) in zip(pairs, carry):
            v_aug = jnp.concatenate([v_ref[pl.ds(c0, ck), v_lane:v_lane + LANES], ones], axis=1)
            me, ae, pve = head(qe_ref, qe_lane, k_ref, kle, c0, bias, v_aug, me)
            mo, ao, pvo = head(qo_ref, qo_lane, k_ref, klo, c0, bias, v_aug, mo)
            a = jnp.where(even, ae, ao)
            l = a * l + jnp.where(even, pve[:, LANES:], pvo[:, LANES:])
            acc = a * acc + jnp.where(even, pve[:, :LANES], pvo[:, :LANES])
            new.append((me, mo, l, acc))
        return tuple(new)

    stat = jnp.full((tq, 1), NEG, F32)
    init = tuple((stat, stat, jnp.zeros((tq, LANES), F32), jnp.zeros((tq, LANES), F32)) for _ in pairs)
    carry = lax.fori_loop(0, n_plain, lambda c, cr: step(c, cr, False), init)
    carry = lax.fori_loop(n_plain, n_kc, lambda c, cr: step(c, cr, True), carry)
    return [acc / l for (_, _, l, acc) in carry]


def _split_bf16(v):
    hi = v.astype(BF16)
    return hi, (v - hi.astype(F32)).astype(BF16)


def _dsa_body(q_ref, k_ref, v_ref, qi_ref, ki_ref, wt_ref, o_ref, s_ref, bias_ref, qe_ref, qo_ref, qih_ref, qil_ref,
              jcut_ref, *, tq, ck, topk):
    i = pl.program_id(1)
    sc = 2 * LANES
    n_sc = (i * tq + tq) // sc
    n_kc = (i * tq + tq + ck - 1) // ck
    seq = s_ref.shape[0]
    kpos = lax.broadcasted_iota(I32, (sc, tq), 0)
    qpos = i * tq + lax.broadcasted_iota(I32, (sc, tq), 1)
    lane = lax.broadcasted_iota(I32, (tq, LANES), 1)
    idx_scale = (B_IDX_DIM * B_IDX_HEADS) ** -0.5

    for h in range(B_IDX_HEADS):
        q2 = qi_ref[:, (h // 2) * LANES:(h // 2 + 1) * LANES]
        in_head = (lane >= B_IDX_DIM) if h % 2 else (lane < B_IDX_DIM)
        hi, lo = _split_bf16(jnp.where(in_head, q2, 0.0))
        qih_ref[:, h * LANES:(h + 1) * LANES] = hi
        qil_ref[:, h * LANES:(h + 1) * LANES] = lo

    def score_chunk(c, carry):
        c0 = pl.multiple_of(c * sc, sc)

        @pl.when(c < n_sc)
        def _():
            k_hi, k_lo = _split_bf16(ki_ref[pl.ds(c0, sc), :])
            score = jnp.zeros((sc, tq), F32)
            for h in range(B_IDX_HEADS):
                q_hi = qih_ref[:, h * LANES:(h + 1) * LANES]
                dots = _nt_dot(k_hi, q_hi) + (_nt_dot(k_hi, qil_ref[:, h * LANES:(h + 1) * LANES]) + _nt_dot(k_lo, q_hi))
                score = score + jnp.maximum(dots, 0.0) * wt_ref[h:h + 1, :]
            score = score * idx_scale
            score = jnp.where(score == 0.0, 0.0, score)
            s_ref[pl.ds(c0, sc), :] = jnp.where(c0 + kpos <= qpos, score, -jnp.inf)

        @pl.when(c >= n_sc)
        def _():
            s_ref[pl.ds(c0, sc), :] = jnp.full((sc, tq), -jnp.inf, F32)

        return carry

    lax.fori_loop(0, n_kc * (ck // sc), score_chunk, 0)

    def count(pred):
        def cb(c, acc):
            c0 = pl.multiple_of(c * sc, sc)
            hit = jnp.where(pred(s_ref[pl.ds(c0, sc), :], c0 + kpos), 1.0, 0.0)
            return acc + jnp.sum(hit.reshape(sc // 8, 8, tq), axis=0)

        return jnp.sum(lax.fori_loop(0, n_sc, cb, jnp.zeros((8, tq), F32)), axis=0, keepdims=True)

    def key_to_f32(key):
        return lax.bitcast_convert_type(key ^ (lax.shift_right_arithmetic(key, 31) & jnp.int32(0x7FFFFFFF)), F32)

    def bit_step(bi, ans):
        cand = ans + lax.shift_left(jnp.int32(1), 31 - bi)
        cand_f = key_to_f32(cand)
        return jnp.where(count(lambda sv, kp: sv >= cand_f) >= float(topk), cand, ans)

    thr = key_to_f32(lax.fori_loop(0, 32, bit_step, jnp.full((1, tq), INT_MIN, I32)))
    few = qpos[0:1, :] < topk

    n_ge = count(lambda sv, kp: sv >= thr)
    need = float(topk) - count(lambda sv, kp: sv > thr)
    jcut_ref[...] = jnp.full(jcut_ref.shape, seq, I32)
    surplus = jnp.where((n_ge > float(topk)) & jnp.logical_not(few), 1.0, 0.0)

    @pl.when(jnp.max(surplus) > 0.0)
    def _():
        nbits = (seq - 1).bit_length()

        def idx_step(bi, ans):
            cand = ans + lax.shift_left(jnp.int32(1), nbits - 1 - bi)
            return jnp.where(count(lambda sv, kp: (sv == thr) & (kp < cand)) < need, cand, ans)

        jcut_ref[...] = jnp.broadcast_to(lax.fori_loop(0, nbits, idx_step, jnp.zeros((1, tq), I32)), jcut_ref.shape)

    thr_sel = jnp.where(few, -jnp.inf, thr)
    jcut = jnp.where(few, -1, jcut_ref[0:1, :])

    def bias_chunk(c, carry):
        c0 = pl.multiple_of(c * sc, sc)
        sv = s_ref[pl.ds(c0, sc), :]
        sel = (sv > thr_sel) | ((sv == thr_sel) & (c0 + kpos <= jcut))
        for r0 in range(0, tq, sc):
            bias_ref[r0:r0 + sc, pl.ds(c0, sc)] = jnp.where(sel, 0.0, NEG)[:, r0:r0 + sc].T
        return carry

    lax.fori_loop(0, n_kc * (ck // sc), bias_chunk, 0)

    q_scale = B_HEAD_DIM ** -0.5 * LOG2E
    for p in range(B_HEADS // 2):
        ls = slice(p * LANES, (p + 1) * LANES)
        q2 = q_ref[:, ls].astype(F32) * q_scale
        qe_ref[:, ls] = jnp.where(lane < B_HEAD_DIM, q2, 0.0).astype(BF16)
        qo_ref[:, ls] = jnp.where(lane >= B_HEAD_DIM, q2, 0.0).astype(BF16)
    bias_fn = lambda c0: bias_ref[:, pl.ds(c0, ck)]
    group = 4
    for p0 in range(0, B_HEADS // 2, group):
        pairs = [(qe_ref, p * LANES, qo_ref, p * LANES, k_ref, p * LANES, p * LANES, v_ref, p * LANES)
                 for p in range(p0, p0 + group)]
        for p, o in zip(range(p0, p0 + group), _flash_pairs(pairs, tq, 0, n_kc, ck, bias_fn)):
            o_ref[:, p * LANES:(p + 1) * LANES] = o.astype(BF16)


def _dsa_mixer(x, w_qkv, w_idx, w_out, g, b, *, batch, seq, tq=256, ck=512):
    m = x.shape[0]
    nq = seq // tq
    hd = B_HEADS * B_HEAD_DIM
    topk = min(B_TOPK_MAX, seq // 4)
    qkv = _matmul(x, w_qkv, tm=512, tn=1024, out_dtype=BF16)
    idx = _matmul(x, w_idx, tm=512, tn=512, out_dtype=F32, precision=HIGHEST)
    w_rows = jnp.pad(idx[:, 3 * LANES:3 * LANES + B_IDX_HEADS].T, ((0, 8 - B_IDX_HEADS), (0, 0)))
    attn = pl.pallas_call(
        functools.partial(_dsa_body, tq=tq, ck=ck, topk=topk),
        name="dsa_attn",
        grid=(batch, nq),
        in_specs=[pl.BlockSpec((tq, hd), lambda bi, i: (bi * nq + i, 0)),
                  pl.BlockSpec((seq, hd), lambda bi, i: (bi, 1)),
                  pl.BlockSpec((seq, hd), lambda bi, i: (bi, 2)),
                  pl.BlockSpec((tq, 2 * LANES), lambda bi, i: (bi * nq + i, 0)),
                  pl.BlockSpec((seq, LANES), lambda bi, i: (bi, 2)),
                  pl.BlockSpec((8, tq), lambda bi, i: (0, bi * nq + i))],
        out_specs=pl.BlockSpec((tq, hd), lambda bi, i: (bi * nq + i, 0)),
        out_shape=jax.ShapeDtypeStruct((m, hd), BF16),
        scratch_shapes=[pltpu.VMEM((seq, tq), F32), pltpu.VMEM((tq, seq), F32), pltpu.VMEM((tq, hd), BF16),
                        pltpu.VMEM((tq, hd), BF16), pltpu.VMEM((tq, B_IDX_HEADS * LANES), BF16),
                        pltpu.VMEM((tq, B_IDX_HEADS * LANES), BF16), pltpu.VMEM((8, tq), I32)],
        compiler_params=_params("parallel", "arbitrary"),
    )(qkv, qkv, qkv, idx, idx, w_rows)
    return _proj_ln(attn, w_out, x, g, b)


def _hgrn2_body(q_ref, f_ref, i_ref, gt_ref, lbl_ref, ng_ref, o_ref, st_ref, *, layer):
    @pl.when(pl.program_id(2) == 0)
    def _():
        st_ref[...] = jnp.zeros_like(st_ref)

    lg = lbl_ref[...]
    e = jnp.exp(lg - jnp.max(lg, axis=0, keepdims=True))
    sm = e / jnp.sum(e, axis=0, keepdims=True)
    lb_all = [sm[0:1]]
    for d in range(1, DEPTH):
        lb_all.append(lb_all[-1] + sm[d:d + 1])
    lb = lb_all[layer] - lb_all[0]
    ng = ng_ref[...]
    rid = lax.broadcasted_iota(I32, (C_SUB, LANES), 0)
    rid8 = lax.broadcasted_iota(I32, (C_SUB // 2, LANES), 0)

    def sub_chunk(j, carry):
        r0 = pl.multiple_of(j * C_SUB, C_SUB)
        for h in range(C_HG):
            ls = slice(h * LANES, (h + 1) * LANES)
            qb = q_ref[pl.ds(r0, C_SUB), ls]
            lbh = lb[:, ls]
            f = lbh + (1.0 - lbh) * jax.nn.sigmoid(f_ref[pl.ds(r0, C_SUB), ls])
            gb = jnp.log(f)
            kb = 1.0 - f
            vb = i_ref[pl.ds(r0, C_SUB), ls]
            gc = gb
            for sh in (1, 2, 4, 8):
                gc = gc + jnp.where(rid >= sh, pltpu.roll(gc, sh, 0), 0.0)
            glast = gc[C_SUB - 1:C_SUB, :]
            st = st_ref[h]
            out = _nt_dot((qb * jnp.exp(gc)).astype(BF16), st.astype(BF16))
            half = C_SUB // 2
            intra = []
            for blk in range(2):
                rows = slice(blk * half, (blk + 1) * half)
                q_b, g_b = qb[rows], gc[rows]
                acc = jnp.zeros((half, LANES), F32)
                for s in range((blk + 1) * half):
                    w = q_b * kb[s:s + 1, :]
                    if s >= blk * half:
                        keep = rid8 >= s - blk * half
                        w = jnp.where(keep, w * jnp.exp(jnp.where(keep, g_b - gc[s:s + 1, :], 0.0)), 0.0)
                    else:
                        w = w * jnp.exp(g_b - gc[s:s + 1, :])
                    acc = acc + jnp.sum(w, axis=1, keepdims=True) * vb[s:s + 1, :]
                intra.append(acc)
            out = out + jnp.concatenate(intra, axis=0)
            kdec = kb * jnp.exp(glast - gc)
            kv_t = lax.dot_general(vb.astype(BF16), kdec.astype(BF16), (((0,), (0,)), ((), ())),
                                   preferred_element_type=F32)
            st_ref[h] = st * jnp.exp(glast) + kv_t
            gate = gt_ref[pl.ds(r0, C_SUB), ls]
            o = out * lax.rsqrt(jnp.mean(out * out, axis=1, keepdims=True) + RMS_EPS) * ng
            o_ref[pl.ds(r0, C_SUB), ls] = (o * (gate * jax.nn.sigmoid(gate))).astype(BF16)
        return carry

    lax.fori_loop(0, q_ref.shape[0] // C_SUB, sub_chunk, 0)


def _hgrn2_mixer(x, w_in, lb_logits, norm_g, w_out, g, b, *, batch, seq, layer):
    m = x.shape[0]
    ts = min(C_TS, seq)
    ns = seq // ts
    w = C_HG * LANES
    nhb = C_HEADS // C_HG
    proj = _matmul(x, w_in, tm=512, tn=1024, out_dtype=F32)
    spec = lambda blk: pl.BlockSpec((ts, w), lambda bi, hb, s: (bi * ns + s, blk * nhb + hb))
    o = pl.pallas_call(
        functools.partial(_hgrn2_body, layer=layer),
        name="hgrn2_scan",
        grid=(batch, nhb, ns),
        in_specs=[spec(0), spec(1), spec(2), spec(3),
                  pl.BlockSpec((DEPTH, w), lambda bi, hb, s: (0, hb)),
                  pl.BlockSpec((1, LANES), lambda bi, hb, s: (0, 0))],
        out_specs=pl.BlockSpec((ts, w), lambda bi, hb, s: (bi * ns + s, hb)),
        out_shape=jax.ShapeDtypeStruct((m, D_MODEL), BF16),
        scratch_shapes=[pltpu.VMEM((C_HG, C_HEAD_V, C_EXPAND), F32)],
        compiler_params=_params("parallel", "parallel", "arbitrary"),
    )(proj, proj, proj, proj, lb_logits, norm_g)
    return _proj_ln(o, w_out, x, g, b)


def _mla_prep_body(p_ref, pos_ref, invf_ref, qg_ref, kvg_ref, wq_ref, wk_ref, wv_ref, q_ref, k_ref, v_ref):
    tm = p_ref.shape[0]
    lane = lax.broadcasted_iota(I32, (tm, LANES), 1)
    ang = pos_ref[...].astype(F32) * invf_ref[...]
    cos_t = jnp.cos(ang)
    sin_t = jnp.sin(ang)
    cq = p_ref[:, 0:D_Q_RANK]
    cq = cq * lax.rsqrt(jnp.mean(cq * cq, axis=1, keepdims=True) + RMS_EPS) * qg_ref[...]
    ckv = p_ref[:, D_Q_RANK:D_Q_RANK + D_KV_RANK]
    ckv = ckv * lax.rsqrt(jnp.mean(ckv * ckv, axis=1, keepdims=True) + RMS_EPS) * kvg_ref[...]
    k_rot = p_ref[:, 3 * LANES:4 * LANES] * cos_t + p_ref[:, 4 * LANES:5 * LANES] * sin_t
    q_tab = jnp.where(lane < D_NOPE + D_ROPE, cos_t, sin_t) * ((D_NOPE + D_ROPE) ** -0.5 * LOG2E)
    cqb = cq.astype(BF16)
    ckvb = ckv.astype(BF16)
    for h in range(D_HEADS):
        ls = slice(h * LANES, (h + 1) * LANES)
        qh = jnp.dot(cqb, wq_ref[:, ls], preferred_element_type=F32)
        q_ref[:, ls] = (qh * q_tab).astype(BF16)
        kh = jnp.dot(ckvb, wk_ref[:, ls], preferred_element_type=F32)
        k_ref[:, ls] = (kh + k_rot).astype(BF16)
    v_ref[...] = jnp.dot(ckvb, wv_ref[...], preferred_element_type=F32).astype(BF16)


def _mla_attn_body(q_ref, k_ref, v_ref, o_ref, *, tq, ck):
    i = pl.program_id(2)
    n_kc = (i * tq + tq + ck - 1) // ck
    n_plain = (i * tq + 1) // ck
    qpos = i * tq + lax.broadcasted_iota(I32, (tq, ck), 0)
    kpos = lax.broadcasted_iota(I32, (tq, ck), 1)
    bias_fn = lambda c0: jnp.where(c0 + kpos <= qpos, 0.0, NEG)
    pairs = [(q_ref, 2 * p * LANES, q_ref, (2 * p + 1) * LANES, k_ref, 2 * p * LANES, (2 * p + 1) * LANES, v_ref, p * LANES)
             for p in range(D_PAIRS)]
    for p, o in enumerate(_flash_pairs(pairs, tq, n_plain, n_kc, ck, bias_fn)):
        o_ref[:, p * LANES:(p + 1) * LANES] = o.astype(BF16)


def _mla_mixer(x, positions, w_in, invf, q_norm_g, kv_norm_g, w_q, w_k, w_v, w_out, g, b, *, batch, seq, tm=256, tq=256, ck=512):
    m = x.shape[0]
    nq = seq // tq
    hw = D_HEADS * LANES
    proj = _matmul(x, w_in, tm=512, tn=w_in.shape[1], out_dtype=F32)
    hv = D_HEADS * D_VDIM
    q, k, v = pl.pallas_call(
        _mla_prep_body,
        name="mla_prep",
        grid=(m // tm,),
        in_specs=[pl.BlockSpec((tm, w_in.shape[1]), lambda i: (i, 0)), pl.BlockSpec((tm, 1), lambda i: (i, 0)),
                  pl.BlockSpec((1, LANES), lambda i: (0, 0)), pl.BlockSpec((1, D_Q_RANK), lambda i: (0, 0)),
                  pl.BlockSpec((1, D_KV_RANK), lambda i: (0, 0)), pl.BlockSpec((D_Q_RANK, hw), lambda i: (0, 0)),
                  pl.BlockSpec((D_KV_RANK, hw), lambda i: (0, 0)),
                  pl.BlockSpec((D_KV_RANK, hv), lambda i: (0, 0))],
        out_specs=[pl.BlockSpec((tm, hw), lambda i: (i, 0)), pl.BlockSpec((tm, hw), lambda i: (i, 0)),
                   pl.BlockSpec((tm, hv), lambda i: (i, 0))],
        out_shape=[jax.ShapeDtypeStruct((m, hw), BF16), jax.ShapeDtypeStruct((m, hw), BF16),
                   jax.ShapeDtypeStruct((m, hv), BF16)],
        compiler_params=_params("parallel"),
    )(proj, positions.reshape(m, 1), invf, q_norm_g, kv_norm_g, w_q, w_k, w_v)
    attn = pl.pallas_call(
        functools.partial(_mla_attn_body, tq=tq, ck=ck),
        name="mla_attn",
        grid=(batch, D_HEADS // (2 * D_PAIRS), nq),
        in_specs=[pl.BlockSpec((tq, 2 * D_PAIRS * LANES), lambda bi, hp, i: (bi * nq + i, hp)),
                  pl.BlockSpec((seq, 2 * D_PAIRS * LANES), lambda bi, hp, i: (bi, hp)),
                  pl.BlockSpec((seq, D_PAIRS * LANES), lambda bi, hp, i: (bi, hp))],
        out_specs=pl.BlockSpec((tq, D_PAIRS * LANES), lambda bi, hp, i: (bi * nq + i, hp)),
        out_shape=jax.ShapeDtypeStruct((m, hv), BF16),
        compiler_params=_params("parallel", "parallel", "arbitrary"),
    )(q, k, v)
    return _proj_ln(attn, w_out, x, g, b)


def _rotate_half_cols(w):
    half = w.shape[-1] // 2
    return jnp.concatenate([-w[..., half:], w[..., :half]], axis=-1)


def _mla_weights(d_w_in, d_w_uq, d_w_ukv):
    zeros = lambda *s: jnp.zeros(s, F32)
    w_kr = d_w_in[:, D_Q_RANK + D_KV_RANK:]
    w_krh = _rotate_half_cols(w_kr)
    z64 = zeros(D_MODEL, D_NOPE)
    w_in = jnp.concatenate([d_w_in[:, :D_Q_RANK + D_KV_RANK], z64, w_kr, w_kr, z64, w_krh, w_krh], axis=1)
    uq = d_w_uq.reshape(D_Q_RANK, D_HEADS, D_NOPE + D_ROPE)
    w_q = jnp.concatenate([uq, _rotate_half_cols(uq[..., D_NOPE:])], axis=-1).reshape(D_Q_RANK, D_HEADS * LANES)
    ukv = d_w_ukv.reshape(D_KV_RANK, D_HEADS, D_NOPE + D_VDIM)
    w_k = jnp.concatenate([ukv[..., :D_NOPE], zeros(D_KV_RANK, D_HEADS, LANES - D_NOPE)], axis=-1)
    w_k = w_k.reshape(D_KV_RANK, D_HEADS * LANES)
    w_v = ukv[..., D_NOPE:].reshape(D_KV_RANK, D_HEADS * D_VDIM)
    half = D_ROPE // 2
    inv_freq = ROPE_BASE ** (-jnp.arange(half, dtype=F32) / half)
    invf = jnp.concatenate([jnp.zeros((D_NOPE,), F32), jnp.tile(inv_freq, 4)]).reshape(1, LANES)
    return w_in.astype(BF16), w_q.astype(BF16), w_k.astype(BF16), w_v.astype(BF16), invf


def _dsa_weights(b_w_in):
    hd = B_HEADS * B_HEAD_DIM
    o = 3 * hd
    nqi = B_IDX_HEADS * B_IDX_DIM
    w_qi = b_w_in[:, o:o + nqi]
    w_ki = b_w_in[:, o + nqi:o + nqi + B_IDX_DIM]
    w_wi = b_w_in[:, o + nqi + B_IDX_DIM:]
    pad = jnp.zeros((D_MODEL, LANES - B_IDX_HEADS), F32)
    w_idx = jnp.concatenate([w_qi, w_ki, w_ki, w_wi, pad], axis=1)
    return b_w_in[:, :o].astype(BF16), w_idx


def kernel(x, positions, a_w_in, a_ln_g, a_ln_b, a_w_s, a_b_s, a_w_out, b_w_in, b_w_out, c_w_in, c_lb_logits, c_norm_g, c_w_out, d_w_in, d_q_norm_g, d_w_uq, d_kv_norm_g, d_w_ukv, d_w_out, ffn0_w_gu, ffn0_w_down, moe1_w_router, moe1_w_gu, moe1_w_down, ffn2_w_gu, ffn2_w_down, moe3_w_router, moe3_w_gu, moe3_w_down, ln_mix_g, ln_mix_b, ln_ffn_g, ln_ffn_b):
    batch, seq, d = x.shape
    m = batch * seq
    row = lambda v: v.reshape(1, -1).astype(F32)
    bf = lambda w: w.astype(BF16)
    h = x.reshape(m, d)

    h = _gmlp_mixer(h, bf(a_w_in), row(a_ln_g), row(a_ln_b), a_w_s, a_b_s.T, bf(a_w_out),
                    row(ln_mix_g[0]), row(ln_mix_b[0]))
    h = _dense_ffn(h, ffn0_w_gu, ffn0_w_down, row(ln_ffn_g[0]), row(ln_ffn_b[0]))

    w_qkv, w_idx = _dsa_weights(b_w_in)
    h = _dsa_mixer(h, w_qkv, w_idx, bf(b_w_out), row(ln_mix_g[1]), row(ln_mix_b[1]), batch=batch, seq=seq)
    h = _moe_ffn(h, moe1_w_router.T, moe1_w_gu, moe1_w_down, row(ln_ffn_g[1]), row(ln_ffn_b[1]))

    h = _hgrn2_mixer(h, bf(c_w_in), c_lb_logits, row(c_norm_g), bf(c_w_out), row(ln_mix_g[2]), row(ln_mix_b[2]),
                     batch=batch, seq=seq, layer=2)
    h = _dense_ffn(h, ffn2_w_gu, ffn2_w_down, row(ln_ffn_g[2]), row(ln_ffn_b[2]))

    w_in, w_q, w_k, w_v, invf = _mla_weights(d_w_in, d_w_uq, d_w_ukv)
    h = _mla_mixer(h, positions, w_in, invf, row(d_q_norm_g), row(d_kv_norm_g), w_q, w_k, w_v, bf(d_w_out),
                   row(ln_mix_g[3]), row(ln_mix_b[3]), batch=batch, seq=seq)
    h = _moe_ffn(h, moe3_w_router.T, moe3_w_gu, moe3_w_down, row(ln_ffn_g[3]), row(ln_ffn_b[3]))
    return h.reshape(batch, seq, d)
```

```python
import functools

import jax
import jax.numpy as jnp
from jax import lax
from jax.experimental import pallas as pl
from jax.experimental.pallas import tpu as pltpu

F32 = jnp.float32
BF16 = jnp.bfloat16
I32 = jnp.int32

D_MODEL = 1024
DEPTH = 4
ALPHA = (2.0 * DEPTH) ** 0.25
LN_EPS = 1e-5
RMS_EPS = 1e-6

A_CHUNK = 128
A_HALF = 2 * D_MODEL
A_GROUPS = 8
A_GROUP_DIM = A_HALF // A_GROUPS

B_HEADS = 16
B_HEAD_DIM = 64
B_IDX_HEADS = 4
B_IDX_DIM = 64
B_TOPK_MAX = 256

C_HEADS = 8
C_EXPAND = 128
C_HEAD_V = 128
C_FDIM = C_HEADS * C_EXPAND
C_SUB = 16
C_HG = 4
C_TS = 512

D_HEADS = 16
D_NOPE = 64
D_ROPE = 32
D_VDIM = 64
D_Q_RANK = 256
D_KV_RANK = 128
D_PAIRS = 4
ROPE_BASE = 10000.0

FFN_DIM = 3584
N_EXPERTS = 8
TOP_K = 2

LANES = 128
VMEM_LIMIT = 56 * 1024 * 1024
NEG = -1e30
LOG2E = 1.4426950408889634
INT_MIN = -2147483648
HIGHEST = lax.Precision.HIGHEST


def _params(*sem):
    return pltpu.CompilerParams(dimension_semantics=sem, vmem_limit_bytes=VMEM_LIMIT)


def _layer_norm(y, g, b):
    mu = jnp.mean(y, axis=-1, keepdims=True)
    d = y - mu
    var = jnp.mean(d * d, axis=-1, keepdims=True)
    return d * lax.rsqrt(var + LN_EPS) * g + b


def _nt_dot(a, b, precision=None):
    return lax.dot_general(a, b, (((1,), (1,)), ((), ())), preferred_element_type=F32, precision=precision)


def _mm_body(a_ref, w_ref, o_ref, *, tn, act, precision):
    a = a_ref[...]
    if precision is None:
        a = a.astype(BF16)
    for n0 in range(0, o_ref.shape[1], tn):
        acc = jnp.dot(a, w_ref[:, n0:n0 + tn], preferred_element_type=F32, precision=precision)
        if act == "gelu":
            acc = jax.nn.gelu(acc)
        o_ref[:, n0:n0 + tn] = acc.astype(o_ref.dtype)


def _matmul(a, w, *, tm, tn, out_dtype, act=None, precision=None):
    m, k = a.shape
    n = w.shape[1]
    return pl.pallas_call(
        functools.partial(_mm_body, tn=tn, act=act, precision=precision),
        name="matmul",
        grid=(m // tm,),
        in_specs=[pl.BlockSpec((tm, k), lambda i: (i, 0)), pl.BlockSpec((k, n), lambda i: (0, 0))],
        out_specs=pl.BlockSpec((tm, n), lambda i: (i, 0)),
        out_shape=jax.ShapeDtypeStruct((m, n), out_dtype),
        compiler_params=_params("parallel"),
    )(a, w)


def _proj_ln_body(a_ref, w_ref, x_ref, g_ref, b_ref, o_ref):
    h = jnp.dot(a_ref[...].astype(BF16), w_ref[...], preferred_element_type=F32)
    o_ref[...] = _layer_norm(ALPHA * x_ref[...] + h, g_ref[...], b_ref[...])


def _proj_ln(a, w, x, g, b, *, tm=512):
    m, k = a.shape
    return pl.pallas_call(
        _proj_ln_body,
        name="proj_ln",
        grid=(m // tm,),
        in_specs=[pl.BlockSpec((tm, k), lambda i: (i, 0)), pl.BlockSpec((k, D_MODEL), lambda i: (0, 0)),
                  pl.BlockSpec((tm, D_MODEL), lambda i: (i, 0)), pl.BlockSpec((1, D_MODEL), lambda i: (0, 0)),
                  pl.BlockSpec((1, D_MODEL), lambda i: (0, 0))],
        out_specs=pl.BlockSpec((tm, D_MODEL), lambda i: (i, 0)),
        out_shape=jax.ShapeDtypeStruct((m, D_MODEL), F32),
        compiler_params=_params("parallel"),
    )(a, w, x, g, b)


def _ffn_body(x_ref, wg_ref, wu_ref, wd_ref, g_ref, b_ref, o_ref, acc_ref, xb_ref):
    j = pl.program_id(1)

    @pl.when(j == 0)
    def _():
        acc_ref[...] = jnp.zeros_like(acc_ref)
        xb_ref[...] = x_ref[...].astype(BF16)

    xb = xb_ref[...]
    gate = jnp.dot(xb, wg_ref[...].astype(BF16), preferred_element_type=F32)
    up = jnp.dot(xb, wu_ref[...].astype(BF16), preferred_element_type=F32)
    h = (gate * jax.nn.sigmoid(gate)) * up
    acc_ref[...] += jnp.dot(h.astype(BF16), wd_ref[...].astype(BF16), preferred_element_type=F32)

    @pl.when(j == pl.num_programs(1) - 1)
    def _():
        o_ref[...] = _layer_norm(ALPHA * x_ref[...] + acc_ref[...], g_ref[...], b_ref[...])


def _dense_ffn(x, w_gu, w_down, g, b, *, tm=1024, tf=512):
    m = x.shape[0]
    nf = FFN_DIM // tf
    return pl.pallas_call(
        _ffn_body,
        name="dense_ffn",
        grid=(m // tm, nf),
        in_specs=[pl.BlockSpec((tm, D_MODEL), lambda i, j: (i, 0)),
                  pl.BlockSpec((D_MODEL, tf), lambda i, j: (0, j)),
                  pl.BlockSpec((D_MODEL, tf), lambda i, j: (0, nf + j)),
                  pl.BlockSpec((tf, D_MODEL), lambda i, j: (j, 0)),
                  pl.BlockSpec((1, D_MODEL), lambda i, j: (0, 0)),
                  pl.BlockSpec((1, D_MODEL), lambda i, j: (0, 0))],
        out_specs=pl.BlockSpec((tm, D_MODEL), lambda i, j: (i, 0)),
        out_shape=jax.ShapeDtypeStruct((m, D_MODEL), F32),
        scratch_shapes=[pltpu.VMEM((tm, D_MODEL), F32), pltpu.VMEM((tm, D_MODEL), BF16)],
        compiler_params=_params("parallel", "arbitrary"),
    )(x, w_gu, w_gu, w_down, g, b)


ROW_TILE = (8, LANES)


def _to_row_tiles(dst_ref, val):
    for k in range(ROW_TILE[0]):
        dst_ref[:, k, :] = val[:, k * LANES:(k + 1) * LANES]


def _from_row_tiles(src_ref, r0, n):
    return jnp.concatenate([src_ref[r0:r0 + n, k, :] for k in range(ROW_TILE[0])], axis=1)


def _router_body(x_ref, wr_ref, e_ref, g_ref):
    logits = _nt_dot(wr_ref[...], x_ref[...], precision=HIGHEST)
    eid = lax.broadcasted_iota(I32, logits.shape, 0).astype(F32)
    m1 = jnp.max(logits, axis=0, keepdims=True)
    i1 = jnp.min(jnp.where(logits == m1, eid, float(N_EXPERTS)), axis=0, keepdims=True)
    rest = jnp.where(eid == i1, -jnp.inf, logits)
    m2 = jnp.max(rest, axis=0, keepdims=True)
    i2 = jnp.min(jnp.where(rest == m2, eid, float(N_EXPERTS)), axis=0, keepdims=True)
    e2 = jnp.exp(m2 - m1)
    den = 1.0 + e2
    e_ref[...] = jnp.concatenate([i1, i2], axis=0).astype(I32)
    g_ref[...] = jnp.concatenate([1.0 / den, e2 / den], axis=0)


def _router(x, w_router_t, *, tm=512):
    m = x.shape[0]
    return pl.pallas_call(
        _router_body,
        name="router",
        grid=(m // tm,),
        in_specs=[pl.BlockSpec((tm, D_MODEL), lambda i: (i, 0)), pl.BlockSpec((N_EXPERTS, D_MODEL), lambda i: (0, 0))],
        out_specs=[pl.BlockSpec((TOP_K, tm), lambda i: (0, i)), pl.BlockSpec((TOP_K, tm), lambda i: (0, i))],
        out_shape=[jax.ShapeDtypeStruct((TOP_K, m), I32), jax.ShapeDtypeStruct((TOP_K, m), F32)],
        compiler_params=_params("parallel"),
    )(x, w_router_t)


def _start_row_gather(idx_ref, src_hbm, dst_ref, sem):
    def start(r, c):
        pltpu.make_async_copy(src_hbm.at[pl.ds(idx_ref[0, 0, r], 1)], dst_ref.at[pl.ds(r, 1)], sem).start()
        return c

    lax.fori_loop(0, dst_ref.shape[0], start, 0, unroll=8)


def _wait_row_gather(src_hbm, dst_ref, sem):
    pltpu.make_async_copy(src_hbm.at[pl.ds(0, dst_ref.shape[0])], dst_ref, sem).wait()


def _moe_body(te_ref, nu_ref, tok_ref, nxt_ref, x_hbm, wg_ref, wu_ref, wd_ref, y_ref, xrow_ref, xb_ref, acc_ref, sem):
    del te_ref
    i = pl.program_id(0)
    j = pl.program_id(1)
    tm = xb_ref.shape[0]
    n_used = nu_ref[0]

    @pl.when((j == 0) & (i < n_used))
    def _():
        slot = lax.rem(i, 2)

        @pl.when(i == 0)
        def _():
            _start_row_gather(tok_ref, x_hbm, xrow_ref.at[0], sem.at[0])

        @pl.when(i + 1 < n_used)
        def _():
            _start_row_gather(nxt_ref, x_hbm, xrow_ref.at[1 - slot], sem.at[1 - slot])

        _wait_row_gather(x_hbm, xrow_ref.at[slot], sem.at[slot])
        xb_ref[...] = xrow_ref[slot].astype(BF16)
        acc_ref[...] = jnp.zeros_like(acc_ref)

    @pl.when(i < n_used)
    def _():
        xb = xb_ref[...]
        gate = jnp.dot(xb, wg_ref[...].astype(BF16), preferred_element_type=F32)
        up = jnp.dot(xb, wu_ref[...].astype(BF16), preferred_element_type=F32)
        h = (gate * jax.nn.sigmoid(gate)) * up
        acc_ref[...] += jnp.dot(h.astype(BF16), wd_ref[...].astype(BF16), preferred_element_type=F32)

    @pl.when(j == pl.num_programs(1) - 1)
    def _():
        @pl.when(i < n_used)
        def _():
            _to_row_tiles(y_ref, acc_ref[...])

        @pl.when(i >= n_used)
        def _():
            y_ref[...] = jnp.zeros_like(y_ref)


def _combine_body(pos_ref, nxt_ref, x_ref, gt_ref, y_hbm, g_ref, b_ref, o_ref, rows_ref, sem):
    i = pl.program_id(0)
    tq = x_ref.shape[0]
    slot = lax.rem(i, 2)

    @pl.when(i == 0)
    def _():
        _start_row_gather(pos_ref, y_hbm, rows_ref.at[0], sem.at[0])

    @pl.when(i + 1 < pl.num_programs(0))
    def _():
        _start_row_gather(nxt_ref, y_hbm, rows_ref.at[1 - slot], sem.at[1 - slot])

    _wait_row_gather(y_hbm, rows_ref.at[slot], sem.at[slot])
    rows = rows_ref.at[slot]
    gt = gt_ref[...]
    h = gt[:, 0:1] * _from_row_tiles(rows, 0, tq) + gt[:, 1:2] * _from_row_tiles(rows, tq, tq)
    o_ref[...] = _layer_norm(ALPHA * x_ref[...] + h, g_ref[...], b_ref[...])


def _moe_ffn(x, w_router_t, w_gu, w_down, g, b, *, tm=1024, tf=512, tq=256):
    m = x.shape[0]
    n_assign = m * TOP_K
    n_tiles = n_assign // tm + N_EXPERTS
    n_rows = n_tiles * tm
    nf = FFN_DIM // tf

    eidx, gate = _router(x, w_router_t)
    flat_e = eidx.reshape(-1)
    onehot = (flat_e[:, None] == jnp.arange(N_EXPERTS, dtype=I32)[None, :]).astype(I32)
    csum = jnp.cumsum(onehot, axis=0)
    rank = jnp.sum(onehot * csum, axis=1) - 1
    counts = csum[-1]
    padded = (counts + tm - 1) // tm * tm
    padded_ends = jnp.cumsum(padded)
    padded_starts = padded_ends - padded
    dest = padded_starts[flat_e] + rank
    tok = jnp.tile(jnp.arange(m, dtype=I32), TOP_K)
    row_tok3 = jnp.zeros((n_rows,), I32).at[dest].set(tok).reshape(n_tiles, 1, tm)
    n_used = (padded_ends[-1:] // tm).astype(I32)
    w_step = lambda i, j, nu: jnp.where(i < nu[0], j, nf - 1)
    tile_start = jnp.arange(n_tiles, dtype=I32) * tm
    tile_e = jnp.minimum(jnp.sum((tile_start[:, None] >= padded_ends[None, :]).astype(I32), axis=1), N_EXPERTS - 1)

    y = pl.pallas_call(
        _moe_body,
        name="moe_ffn",
        grid_spec=pltpu.PrefetchScalarGridSpec(
            num_scalar_prefetch=2,
            grid=(n_tiles, nf),
            in_specs=[pl.BlockSpec((1, 1, tm), lambda i, j, te, nu: (i, 0, 0), memory_space=pltpu.SMEM),
                      pl.BlockSpec((1, 1, tm), lambda i, j, te, nu: (jnp.minimum(i + 1, n_tiles - 1), 0, 0),
                                   memory_space=pltpu.SMEM),
                      pl.BlockSpec(memory_space=pl.ANY),
                      pl.BlockSpec((None, D_MODEL, tf), lambda i, j, te, nu: (te[i], 0, w_step(i, j, nu))),
                      pl.BlockSpec((None, D_MODEL, tf), lambda i, j, te, nu: (te[i], 0, nf + w_step(i, j, nu))),
                      pl.BlockSpec((None, tf, D_MODEL), lambda i, j, te, nu: (te[i], w_step(i, j, nu), 0))],
            out_specs=pl.BlockSpec((tm,) + ROW_TILE, lambda i, j, te, nu: (i, 0, 0)),
            scratch_shapes=[pltpu.VMEM((2, tm, D_MODEL), F32), pltpu.VMEM((tm, D_MODEL), BF16),
                            pltpu.VMEM((tm, D_MODEL), F32), pltpu.SemaphoreType.DMA((2,))]),
        out_shape=jax.ShapeDtypeStruct((n_rows,) + ROW_TILE, F32),
        compiler_params=_params("arbitrary", "arbitrary"),
    )(tile_e, n_used, row_tok3, row_tok3, x, w_gu, w_gu, w_down)

    pos = jnp.concatenate([dest[:m].reshape(m // tq, 1, tq), dest[m:].reshape(m // tq, 1, tq)], axis=2)
    return pl.pallas_call(
        _combine_body,
        name="moe_combine",
        grid=(m // tq,),
        in_specs=[pl.BlockSpec((1, 1, 2 * tq), lambda i: (i, 0, 0), memory_space=pltpu.SMEM),
                  pl.BlockSpec((1, 1, 2 * tq), lambda i: (jnp.minimum(i + 1, m // tq - 1), 0, 0),
                               memory_space=pltpu.SMEM),
                  pl.BlockSpec((tq, D_MODEL), lambda i: (i, 0)),
                  pl.BlockSpec((tq, TOP_K), lambda i: (i, 0)),
                  pl.BlockSpec(memory_space=pl.ANY),
                  pl.BlockSpec((1, D_MODEL), lambda i: (0, 0)),
                  pl.BlockSpec((1, D_MODEL), lambda i: (0, 0))],
        out_specs=pl.BlockSpec((tq, D_MODEL), lambda i: (i, 0)),
        out_shape=jax.ShapeDtypeStruct((m, D_MODEL), F32),
        scratch_shapes=[pltpu.VMEM((2, 2 * tq) + ROW_TILE, F32), pltpu.SemaphoreType.DMA((2,))],
        compiler_params=_params("arbitrary"),
    )(pos, pos, x, gate.T, y, g, b)


def _gmlp_gate_body(u_ref, v_ref, lg_ref, lb_ref, ws_ref, bs_ref, o_ref):
    tm = u_ref.shape[0]
    v = _layer_norm(v_ref[...].astype(F32), lg_ref[...], lb_ref[...]).astype(BF16)
    r = lax.broadcasted_iota(I32, (A_CHUNK, A_CHUNK), 0)
    c = lax.broadcasted_iota(I32, (A_CHUNK, A_CHUNK), 1)
    causal = c <= r
    bs = bs_ref[...]
    for grp in range(A_GROUPS):
        wc = jnp.where(causal, ws_ref[grp], 0.0).astype(BF16)
        lo = grp * A_GROUP_DIM
        for ch in range(tm // A_CHUNK):
            r0 = ch * A_CHUNK
            s = jnp.dot(wc, v[r0:r0 + A_CHUNK, lo:lo + A_GROUP_DIM], preferred_element_type=F32)
            s = s + bs[:, grp:grp + 1]
            u = u_ref[r0:r0 + A_CHUNK, lo:lo + A_GROUP_DIM].astype(F32)
            o_ref[r0:r0 + A_CHUNK, lo:lo + A_GROUP_DIM] = (u * s).astype(BF16)


def _gmlp_mixer(x, w_in, ln_g, ln_b, w_s, b_s_t, w_out, g, b, *, tm=256):
    m = x.shape[0]
    uv = _matmul(x, w_in, tm=512, tn=1024, out_dtype=BF16, act="gelu")
    gated = pl.pallas_call(
        _gmlp_gate_body,
        name="gmlp_gate",
        grid=(m // tm,),
        in_specs=[pl.BlockSpec((tm, A_HALF), lambda i: (i, 0)), pl.BlockSpec((tm, A_HALF), lambda i: (i, 1)),
                  pl.BlockSpec((1, A_HALF), lambda i: (0, 0)), pl.BlockSpec((1, A_HALF), lambda i: (0, 0)),
                  pl.BlockSpec((A_GROUPS, A_CHUNK, A_CHUNK), lambda i: (0, 0, 0)),
                  pl.BlockSpec((A_CHUNK, A_GROUPS), lambda i: (0, 0))],
        out_specs=pl.BlockSpec((tm, A_HALF), lambda i: (i, 0)),
        out_shape=jax.ShapeDtypeStruct((m, A_HALF), BF16),
        compiler_params=_params("parallel"),
    )(uv, uv, ln_g, ln_b, w_s, b_s_t)
    return _proj_ln(gated, w_out, x, g, b)


def _flash_pairs(pairs, tq, n_plain, n_kc, ck, bias_fn):
    even = lax.broadcasted_iota(I32, (tq, LANES), 1) < (LANES // 2)
    ones = jnp.ones((ck, LANES), BF16)

    def head(q_ref, q_lane, k_ref, k_lane, c0, bias, v_aug, m):
        s = _nt_dot(q_ref[:, q_lane:q_lane + LANES], k_ref[pl.ds(c0, ck), k_lane:k_lane + LANES])
        if bias is not None:
            s = s + bias
        m_new = jnp.maximum(m, jnp.max(s, axis=1, keepdims=True))
        p = jnp.exp2(s - m_new).astype(BF16)
        pv = jnp.dot(p, v_aug, preferred_element_type=F32)
        return m_new, jnp.exp2(m - m_new), pv

    def step(c, carry, masked):
        c0 = pl.multiple_of(c * ck, ck)
        bias = bias_fn(c0) if masked else None
        new = []
        for (qe_ref, qe_lane, qo_ref, qo_lane, k_ref, kle, klo, v_ref, v_lane), (me, mo, l, acc) in zip(pairs, carry):
            v_aug = jnp.concatenate([v_ref[pl.ds(c0, ck), v_lane:v_lane + LANES], ones], axis=1)
            me, ae, pve = head(qe_ref, qe_lane, k_ref, kle, c0, bias, v_aug, me)
            mo, ao, pvo = head(qo_ref, qo_lane, k_ref, klo, c0, bias, v_aug, mo)
            a = jnp.where(even, ae, ao)
            l = a * l + jnp.where(even, pve[:, LANES:], pvo[:, LANES:])
            acc = a * acc + jnp.where(even, pve[:, :LANES], pvo[:, :LANES])
            new.append((me, mo, l, acc))
        return tuple(new)

    stat = jnp.full((tq, 1), NEG, F32)
    init = tuple((stat, stat, jnp.zeros((tq, LANES), F32), jnp.zeros((tq, LANES), F32)) for _ in pairs)
    carry = lax.fori_loop(0, n_plain, lambda c, cr: step(c, cr, False), init)
    carry = lax.fori_loop(n_plain, n_kc, lambda c, cr: step(c, cr, True), carry)
    return [acc / l for (_, _, l, acc) in carry]


def _split_bf16(v):
    hi = v.astype(BF16)
    return hi, (v - hi.astype(F32)).astype(BF16)


def _dsa_body(q_ref, k_ref, v_ref, qi_ref, ki_ref, wt_ref, o_ref, s_ref, bias_ref, qe_ref, qo_ref, qih_ref, qil_ref,
              jcut_ref, *, tq, ck, topk):
    i = pl.program_id(1)
    sc = 2 * LANES
    n_sc = (i * tq + tq) // sc
    n_kc = (i * tq + tq + ck - 1) // ck
    seq = s_ref.shape[0]
    kpos = lax.broadcasted_iota(I32, (sc, tq), 0)
    qpos = i * tq + lax.broadcasted_iota(I32, (sc, tq), 1)
    lane = lax.broadcasted_iota(I32, (tq, LANES), 1)
    idx_scale = (B_IDX_DIM * B_IDX_HEADS) ** -0.5

    for h in range(B_IDX_HEADS):
        q2 = qi_ref[:, (h // 2) * LANES:(h // 2 + 1) * LANES]
        in_head = (lane >= B_IDX_DIM) if h % 2 else (lane < B_IDX_DIM)
        hi, lo = _split_bf16(jnp.where(in_head, q2, 0.0))
        qih_ref[:, h * LANES:(h + 1) * LANES] = hi
        qil_ref[:, h * LANES:(h + 1) * LANES] = lo

    def score_chunk(c, carry):
        c0 = pl.multiple_of(c * sc, sc)

        @pl.when(c < n_sc)
        def _():
            k_hi, k_lo = _split_bf16(ki_ref[pl.ds(c0, sc), :])
            score = jnp.zeros((sc, tq), F32)
            for h in range(B_IDX_HEADS):
                q_hi = qih_ref[:, h * LANES:(h + 1) * LANES]
                dots = _nt_dot(k_hi, q_hi) + (_nt_dot(k_hi, qil_ref[:, h * LANES:(h + 1) * LANES]) + _nt_dot(k_lo, q_hi))
                score = score + jnp.maximum(dots, 0.0) * wt_ref[h:h + 1, :]
            score = score * idx_scale
            score = jnp.where(score == 0.0, 0.0, score)
            s_ref[pl.ds(c0, sc), :] = jnp.where(c0 + kpos <= qpos, score, -jnp.inf)

        @pl.when(c >= n_sc)
        def _():
            s_ref[pl.ds(c0, sc), :] = jnp.full((sc, tq), -jnp.inf, F32)

        return carry

    lax.fori_loop(0, n_kc * (ck // sc), score_chunk, 0)

    def count(pred):
        def cb(c, acc):
            c0 = pl.multiple_of(c * sc, sc)
            hit = jnp.where(pred(s_ref[pl.ds(c0, sc), :], c0 + kpos), 1.0, 0.0)
            return acc + jnp.sum(hit.reshape(sc // 8, 8, tq), axis=0)

        return jnp.sum(lax.fori_loop(0, n_sc, cb, jnp.zeros((8, tq), F32)), axis=0, keepdims=True)

    def key_to_f32(key):
        return lax.bitcast_convert_type(key ^ (lax.shift_right_arithmetic(key, 31) & jnp.int32(0x7FFFFFFF)), F32)

    def bit_step(bi, ans):
        cand = ans + lax.shift_left(jnp.int32(1), 31 - bi)
        cand_f = key_to_f32(cand)
        return jnp.where(count(lambda sv, kp: sv >= cand_f) >= float(topk), cand, ans)

    thr = key_to_f32(lax.fori_loop(0, 32, bit_step, jnp.full((1, tq), INT_MIN, I32)))
    few = qpos[0:1, :] < topk

    n_ge = count(lambda sv, kp: sv >= thr)
    need = float(topk) - count(lambda sv, kp: sv > thr)
    jcut_ref[...] = jnp.full(jcut_ref.shape, seq, I32)
    surplus = jnp.where((n_ge > float(topk)) & jnp.logical_not(few), 1.0, 0.0)

    @pl.when(jnp.max(surplus) > 0.0)
    def _():
        nbits = (seq - 1).bit_length()

        def idx_step(bi, ans):
            cand = ans + lax.shift_left(jnp.int32(1), nbits - 1 - bi)
            return jnp.where(count(lambda sv, kp: (sv == thr) & (kp < cand)) < need, cand, ans)

        jcut_ref[...] = jnp.broadcast_to(lax.fori_loop(0, nbits, idx_step, jnp.zeros((1, tq), I32)), jcut_ref.shape)

    thr_sel = jnp.where(few, -jnp.inf, thr)
    jcut = jnp.where(few, -1, jcut_ref[0:1, :])

    def bias_chunk(c, carry):
        c0 = pl.multiple_of(c * sc, sc)
        sv = s_ref[pl.ds(c0, sc), :]
        sel = (sv > thr_sel) | ((sv == thr_sel) & (c0 + kpos <= jcut))
        for r0 in range(0, tq, sc):
            bias_ref[r0:r0 + sc, pl.ds(c0, sc)] = jnp.where(sel, 0.0, NEG)[:, r0:r0 + sc].T
        return carry

    lax.fori_loop(0, n_kc * (ck // sc), bias_chunk, 0)

    q_scale = B_HEAD_DIM ** -0.5 * LOG2E
    for p in range(B_HEADS // 2):
        ls = slice(p * LANES, (p + 1) * LANES)
        q2 = q_ref[:, ls].astype(F32) * q_scale
        qe_ref[:, ls] = jnp.where(lane < B_HEAD_DIM, q2, 0.0).astype(BF16)
        qo_ref[:, ls] = jnp.where(lane >= B_HEAD_DIM, q2, 0.0).astype(BF16)
    bias_fn = lambda c0: bias_ref[:, pl.ds(c0, ck)]
    group = 4
    for p0 in range(0, B_HEADS // 2, group):
        pairs = [(qe_ref, p * LANES, qo_ref, p * LANES, k_ref, p * LANES, p * LANES, v_ref, p * LANES)
                 for p in range(p0, p0 + group)]
        for p, o in zip(range(p0, p0 + group), _flash_pairs(pairs, tq, 0, n_kc, ck, bias_fn)):
            o_ref[:, p * LANES:(p + 1) * LANES] = o.astype(BF16)


def _dsa_mixer(x, w_qkv, w_idx, w_out, g, b, *, batch, seq, tq=256, ck=512):
    m = x.shape[0]
    nq = seq // tq
    hd = B_HEADS * B_HEAD_DIM
    topk = min(B_TOPK_MAX, seq // 4)
    qkv = _matmul(x, w_qkv, tm=512, tn=1024, out_dtype=BF16)
    idx = _matmul(x, w_idx, tm=512, tn=512, out_dtype=F32, precision=HIGHEST)
    w_rows = jnp.pad(idx[:, 3 * LANES:3 * LANES + B_IDX_HEADS].T, ((0, 8 - B_IDX_HEADS), (0, 0)))
    attn = pl.pallas_call(
        functools.partial(_dsa_body, tq=tq, ck=ck, topk=topk),
        name="dsa_attn",
        grid=(batch, nq),
        in_specs=[pl.BlockSpec((tq, hd), lambda bi, i: (bi * nq + i, 0)),
                  pl.BlockSpec((seq, hd), lambda bi, i: (bi, 1)),
                  pl.BlockSpec((seq, hd), lambda bi, i: (bi, 2)),
                  pl.BlockSpec((tq, 2 * LANES), lambda bi, i: (bi * nq + i, 0)),
                  pl.BlockSpec((seq, LANES), lambda bi, i: (bi, 2)),
                  pl.BlockSpec((8, tq), lambda bi, i: (0, bi * nq + i))],
        out_specs=pl.BlockSpec((tq, hd), lambda bi, i: (bi * nq + i, 0)),
        out_shape=jax.ShapeDtypeStruct((m, hd), BF16),
        scratch_shapes=[pltpu.VMEM((seq, tq), F32), pltpu.VMEM((tq, seq), F32), pltpu.VMEM((tq, hd), BF16),
                        pltpu.VMEM((tq, hd), BF16), pltpu.VMEM((tq, B_IDX_HEADS * LANES), BF16),
                        pltpu.VMEM((tq, B_IDX_HEADS * LANES), BF16), pltpu.VMEM((8, tq), I32)],
        compiler_params=_params("parallel", "arbitrary"),
    )(qkv, qkv, qkv, idx, idx, w_rows)
    return _proj_ln(attn, w_out, x, g, b)


def _hgrn2_body(q_ref, f_ref, i_ref, gt_ref, lbl_ref, ng_ref, o_ref, st_ref, *, layer):
    @pl.when(pl.program_id(2) == 0)
    def _():
        st_ref[...] = jnp.zeros_like(st_ref)

    lg = lbl_ref[...]
    e = jnp.exp(lg - jnp.max(lg, axis=0, keepdims=True))
    sm = e / jnp.sum(e, axis=0, keepdims=True)
    lb_all = [sm[0:1]]
    for d in range(1, DEPTH):
        lb_all.append(lb_all[-1] + sm[d:d + 1])
    lb = lb_all[layer] - lb_all[0]
    ng = ng_ref[...]
    rid = lax.broadcasted_iota(I32, (C_SUB, LANES), 0)
    rid8 = lax.broadcasted_iota(I32, (C_SUB // 2, LANES), 0)

    def sub_chunk(j, carry):
        r0 = pl.multiple_of(j * C_SUB, C_SUB)
        for h in range(C_HG):
            ls = slice(h * LANES, (h + 1) * LANES)
            qb = q_ref[pl.ds(r0, C_SUB), ls]
            lbh = lb[:, ls]
            f = lbh + (1.0 - lbh) * jax.nn.sigmoid(f_ref[pl.ds(r0, C_SUB), ls])
            gb = jnp.log(f)
            kb = 1.0 - f
            vb = i_ref[pl.ds(r0, C_SUB), ls]
            gc = gb
            for sh in (1, 2, 4, 8):
                gc = gc + jnp.where(rid >= sh, pltpu.roll(gc, sh, 0), 0.0)
            glast = gc[C_SUB - 1:C_SUB, :]
            st = st_ref[h]
            out = _nt_dot((qb * jnp.exp(gc)).astype(BF16), st.astype(BF16))
            half = C_SUB // 2
            intra = []
            for blk in range(2):
                rows = slice(blk * half, (blk + 1) * half)
                q_b, g_b = qb[rows], gc[rows]
                acc = jnp.zeros((half, LANES), F32)
                for s in range((blk + 1) * half):
                    w = q_b * kb[s:s + 1, :]
                    if s >= blk * half:
                        keep = rid8 >= s - blk * half
                        w = jnp.where(keep, w * jnp.exp(jnp.where(keep, g_b - gc[s:s + 1, :], 0.0)), 0.0)
                    else:
                        w = w * jnp.exp(g_b - gc[s:s + 1, :])
                    acc = acc + jnp.sum(w, axis=1, keepdims=True) * vb[s:s + 1, :]
                intra.append(acc)
            out = out + jnp.concatenate(intra, axis=0)
            kdec = kb * jnp.exp(glast - gc)
            kv_t = lax.dot_general(vb.astype(BF16), kdec.astype(BF16), (((0,), (0,)), ((), ())),
                                   preferred_element_type=F32)
            st_ref[h] = st * jnp.exp(glast) + kv_t
            gate = gt_ref[pl.ds(r0, C_SUB), ls]
            o = out * lax.rsqrt(jnp.mean(out * out, axis=1, keepdims=True) + RMS_EPS) * ng
            o_ref[pl.ds(r0, C_SUB), ls] = (o * (gate * jax.nn.sigmoid(gate))).astype(BF16)
        return carry

    lax.fori_loop(0, q_ref.shape[0] // C_SUB, sub_chunk, 0)


def _hgrn2_mixer(x, w_in, lb_logits, norm_g, w_out, g, b, *, batch, seq, layer):
    m = x.shape[0]
    ts = min(C_TS, seq)
    ns = seq // ts
    w = C_HG * LANES
    nhb = C_HEADS // C_HG
    proj = _matmul(x, w_in, tm=512, tn=1024, out_dtype=F32)
    spec = lambda blk: pl.BlockSpec((ts, w), lambda bi, hb, s: (bi * ns + s, blk * nhb + hb))
    o = pl.pallas_call(
        functools.partial(_hgrn2_body, layer=layer),
        name="hgrn2_scan",
        grid=(batch, nhb, ns),
        in_specs=[spec(0), spec(1), spec(2), spec(3),
                  pl.BlockSpec((DEPTH, w), lambda bi, hb, s: (0, hb)),
                  pl.BlockSpec((1, LANES), lambda bi, hb, s: (0, 0))],
        out_specs=pl.BlockSpec((ts, w), lambda bi, hb, s: (bi * ns + s, hb)),
        out_shape=jax.ShapeDtypeStruct((m, D_MODEL), BF16),
        scratch_shapes=[pltpu.VMEM((C_HG, C_HEAD_V, C_EXPAND), F32)],
        compiler_params=_params("parallel", "parallel", "arbitrary"),
    )(proj, proj, proj, proj, lb_logits, norm_g)
    return _proj_ln(o, w_out, x, g, b)


def _mla_prep_body(p_ref, pos_ref, invf_ref, qg_ref, kvg_ref, wq_ref, wk_ref, wv_ref, q_ref, k_ref, v_ref):
    tm = p_ref.shape[0]
    lane = lax.broadcasted_iota(I32, (tm, LANES), 1)
    ang = pos_ref[...].astype(F32) * invf_ref[...]
    cos_t = jnp.cos(ang)
    sin_t = jnp.sin(ang)
    cq = p_ref[:, 0:D_Q_RANK]
    cq = cq * lax.rsqrt(jnp.mean(cq * cq, axis=1, keepdims=True) + RMS_EPS) * qg_ref[...]
    ckv = p_ref[:, D_Q_RANK:D_Q_RANK + D_KV_RANK]
    ckv = ckv * lax.rsqrt(jnp.mean(ckv * ckv, axis=1, keepdims=True) + RMS_EPS) * kvg_ref[...]
    k_rot = p_ref[:, 3 * LANES:4 * LANES] * cos_t + p_ref[:, 4 * LANES:5 * LANES] * sin_t
    q_tab = jnp.where(lane < D_NOPE + D_ROPE, cos_t, sin_t) * ((D_NOPE + D_ROPE) ** -0.5 * LOG2E)
    cqb = cq.astype(BF16)
    ckvb = ckv.astype(BF16)
    for h in range(D_HEADS):
        ls = slice(h * LANES, (h + 1) * LANES)
        qh = jnp.dot(cqb, wq_ref[:, ls], preferred_element_type=F32)
        q_ref[:, ls] = (qh * q_tab).astype(BF16)
        kh = jnp.dot(ckvb, wk_ref[:, ls], preferred_element_type=F32)
        k_ref[:, ls] = (kh + k_rot).astype(BF16)
    v_ref[...] = jnp.dot(ckvb, wv_ref[...], preferred_element_type=F32).astype(BF16)


def _mla_attn_body(q_ref, k_ref, v_ref, o_ref, *, tq, ck):
    i = pl.program_id(2)
    n_kc = (i * tq + tq + ck - 1) // ck
    n_plain = (i * tq + 1) // ck
    qpos = i * tq + lax.broadcasted_iota(I32, (tq, ck), 0)
    kpos = lax.broadcasted_iota(I32, (tq, ck), 1)
    bias_fn = lambda c0: jnp.where(c0 + kpos <= qpos, 0.0, NEG)
    pairs = [(q_ref, 2 * p * LANES, q_ref, (2 * p + 1) * LANES, k_ref, 2 * p * LANES, (2 * p + 1) * LANES, v_ref, p * LANES)
             for p in range(D_PAIRS)]
    for p, o in enumerate(_flash_pairs(pairs, tq, n_plain, n_kc, ck, bias_fn)):
        o_ref[:, p * LANES:(p + 1) * LANES] = o.astype(BF16)


def _mla_mixer(x, positions, w_in, invf, q_norm_g, kv_norm_g, w_q, w_k, w_v, w_out, g, b, *, batch, seq, tm=256, tq=256, ck=512):
    m = x.shape[0]
    nq = seq // tq
    hw = D_HEADS * LANES
    proj = _matmul(x, w_in, tm=512, tn=w_in.shape[1], out_dtype=F32)
    hv = D_HEADS * D_VDIM
    q, k, v = pl.pallas_call(
        _mla_prep_body,
        name="mla_prep",
        grid=(m // tm,),
        in_specs=[pl.BlockSpec((tm, w_in.shape[1]), lambda i: (i, 0)), pl.BlockSpec((tm, 1), lambda i: (i, 0)),
                  pl.BlockSpec((1, LANES), lambda i: (0, 0)), pl.BlockSpec((1, D_Q_RANK), lambda i: (0, 0)),
                  pl.BlockSpec((1, D_KV_RANK), lambda i: (0, 0)), pl.BlockSpec((D_Q_RANK, hw), lambda i: (0, 0)),
                  pl.BlockSpec((D_KV_RANK, hw), lambda i: (0, 0)),
                  pl.BlockSpec((D_KV_RANK, hv), lambda i: (0, 0))],
        out_specs=[pl.BlockSpec((tm, hw), lambda i: (i, 0)), pl.BlockSpec((tm, hw), lambda i: (i, 0)),
                   pl.BlockSpec((tm, hv), lambda i: (i, 0))],
        out_shape=[jax.ShapeDtypeStruct((m, hw), BF16), jax.ShapeDtypeStruct((m, hw), BF16),
                   jax.ShapeDtypeStruct((m, hv), BF16)],
        compiler_params=_params("parallel"),
    )(proj, positions.reshape(m, 1), invf, q_norm_g, kv_norm_g, w_q, w_k, w_v)
    attn = pl.pallas_call(
        functools.partial(_mla_attn_body, tq=tq, ck=ck),
        name="mla_attn",
        grid=(batch, D_HEADS // (2 * D_PAIRS), nq),
        in_specs=[pl.BlockSpec((tq, 2 * D_PAIRS * LANES), lambda bi, hp, i: (bi * nq + i, hp)),
                  pl.BlockSpec((seq, 2 * D_PAIRS * LANES), lambda bi, hp, i: (bi, hp)),
                  pl.BlockSpec((seq, D_PAIRS * LANES), lambda bi, hp, i: (bi, hp))],
        out_specs=pl.BlockSpec((tq, D_PAIRS * LANES), lambda bi, hp, i: (bi * nq + i, hp)),
        out_shape=jax.ShapeDtypeStruct((m, hv), BF16),
        compiler_params=_params("parallel", "parallel", "arbitrary"),
    )(q, k, v)
    return _proj_ln(attn, w_out, x, g, b)


def _rotate_half_cols(w):
    half = w.shape[-1] // 2
    return jnp.concatenate([-w[..., half:], w[..., :half]], axis=-1)


def _mla_weights(d_w_in, d_w_uq, d_w_ukv):
    zeros = lambda *s: jnp.zeros(s, F32)
    w_kr = d_w_in[:, D_Q_RANK + D_KV_RANK:]
    w_krh = _rotate_half_cols(w_kr)
    z64 = zeros(D_MODEL, D_NOPE)
    w_in = jnp.concatenate([d_w_in[:, :D_Q_RANK + D_KV_RANK], z64, w_kr, w_kr, z64, w_krh, w_krh], axis=1)
    uq = d_w_uq.reshape(D_Q_RANK, D_HEADS, D_NOPE + D_ROPE)
    w_q = jnp.concatenate([uq, _rotate_half_cols(uq[..., D_NOPE:])], axis=-1).reshape(D_Q_RANK, D_HEADS * LANES)
    ukv = d_w_ukv.reshape(D_KV_RANK, D_HEADS, D_NOPE + D_VDIM)
    w_k = jnp.concatenate([ukv[..., :D_NOPE], zeros(D_KV_RANK, D_HEADS, LANES - D_NOPE)], axis=-1)
    w_k = w_k.reshape(D_KV_RANK, D_HEADS * LANES)
    w_v = ukv[..., D_NOPE:].reshape(D_KV_RANK, D_HEADS * D_VDIM)
    half = D_ROPE // 2
    inv_freq = ROPE_BASE ** (-jnp.arange(half, dtype=F32) / half)
    invf = jnp.concatenate([jnp.zeros((D_NOPE,), F32), jnp.tile(inv_freq, 4)]).reshape(1, LANES)
    return w_in.astype(BF16), w_q.astype(BF16), w_k.astype(BF16), w_v.astype(BF16), invf


def _dsa_weights(b_w_in):
    hd = B_HEADS * B_HEAD_DIM
    o = 3 * hd
    nqi = B_IDX_HEADS * B_IDX_DIM
    w_qi = b_w_in[:, o:o + nqi]
    w_ki = b_w_in[:, o + nqi:o + nqi + B_IDX_DIM]
    w_wi = b_w_in[:, o + nqi + B_IDX_DIM:]
    pad = jnp.zeros((D_MODEL, LANES - B_IDX_HEADS), F32)
    w_idx = jnp.concatenate([w_qi, w_ki, w_ki, w_wi, pad], axis=1)
    return b_w_in[:, :o].astype(BF16), w_idx


def kernel(x, positions, a_w_in, a_ln_g, a_ln_b, a_w_s, a_b_s, a_w_out, b_w_in, b_w_out, c_w_in, c_lb_logits, c_norm_g, c_w_out, d_w_in, d_q_norm_g, d_w_uq, d_kv_norm_g, d_w_ukv, d_w_out, ffn0_w_gu, ffn0_w_down, moe1_w_router, moe1_w_gu, moe1_w_down, ffn2_w_gu, ffn2_w_down, moe3_w_router, moe3_w_gu, moe3_w_down, ln_mix_g, ln_mix_b, ln_ffn_g, ln_ffn_b):
    batch, seq, d = x.shape
    m = batch * seq
    row = lambda v: v.reshape(1, -1).astype(F32)
    bf = lambda w: w.astype(BF16)
    h = x.reshape(m, d)

    h = _gmlp_mixer(h, bf(a_w_in), row(a_ln_g), row(a_ln_b), a_w_s, a_b_s.T, bf(a_w_out),
                    row(ln_mix_g[0]), row(ln_mix_b[0]))
    h = _dense_ffn(h, ffn0_w_gu, ffn0_w_down, row(ln_ffn_g[0]), row(ln_ffn_b[0]))

    w_qkv, w_idx = _dsa_weights(b_w_in)
    h = _dsa_mixer(h, w_qkv, w_idx, bf(b_w_out), row(ln_mix_g[1]), row(ln_mix_b[1]), batch=batch, seq=seq)
    h = _moe_ffn(h, moe1_w_router.T, moe1_w_gu, moe1_w_down, row(ln_ffn_g[1]), row(ln_ffn_b[1]))

    h = _hgrn2_mixer(h, bf(c_w_in), c_lb_logits, row(c_norm_g), bf(c_w_out), row(ln_mix_g[2]), row(ln_mix_b[2]),
                     batch=batch, seq=seq, layer=2)
    h = _dense_ffn(h, ffn2_w_gu, ffn2_w_down, row(ln_ffn_g[2]), row(ln_ffn_b[2]))

    w_in, w_q, w_k, w_v, invf = _mla_weights(d_w_in, d_w_uq, d_w_ukv)
    h = _mla_mixer(h, positions, w_in, invf, row(d_q_norm_g), row(d_kv_norm_g), w_q, w_k, w_v, bf(d_w_out),
                   row(ln_mix_g[3]), row(ln_mix_b[3]), batch=batch, seq=seq)
    h = _moe_ffn(h, moe3_w_router.T, moe3_w_gu, moe3_w_down, row(ln_ffn_g[3]), row(ln_ffn_b[3]))
    return h.reshape(batch, seq, d)
```

```python
import functools

import jax
import jax.numpy as jnp
from jax import lax
from jax.experimental import pallas as pl
from jax.experimental.pallas import tpu as pltpu

F32 = jnp.float32
BF16 = jnp.bfloat16
I32 = jnp.int32

D_MODEL = 1024
DEPTH = 4
ALPHA = (2.0 * DEPTH) ** 0.25
LN_EPS = 1e-5
RMS_EPS = 1e-6

A_CHUNK = 128
A_HALF = 2 * D_MODEL
A_GROUPS = 8
A_GROUP_DIM = A_HALF // A_GROUPS

B_HEADS = 16
B_HEAD_DIM = 64
B_IDX_HEADS = 4
B_IDX_DIM = 64
B_TOPK_MAX = 256

C_HEADS = 8
C_EXPAND = 128
C_HEAD_V = 128
C_FDIM = C_HEADS * C_EXPAND
C_SUB = 16
C_HG = 4
C_TS = 1024

D_HEADS = 16
D_NOPE = 64
D_ROPE = 32
D_VDIM = 64
D_Q_RANK = 256
D_KV_RANK = 128
D_PAIRS = 8
ROPE_BASE = 10000.0

FFN_DIM = 3584
N_EXPERTS = 8
TOP_K = 2

LANES = 128
VMEM_LIMIT = 56 * 1024 * 1024
NEG = -1e30
LOG2E = 1.4426950408889634
INT_MIN = -2147483648
HIGHEST = lax.Precision.HIGHEST


def _params(*sem):
    return pltpu.CompilerParams(dimension_semantics=sem, vmem_limit_bytes=VMEM_LIMIT)


def _layer_norm(y, g, b):
    mu = jnp.mean(y, axis=-1, keepdims=True)
    d = y - mu
    var = jnp.mean(d * d, axis=-1, keepdims=True)
    return d * lax.rsqrt(var + LN_EPS) * g + b


def _nt_dot(a, b, precision=None):
    return lax.dot_general(a, b, (((1,), (1,)), ((), ())), preferred_element_type=F32, precision=precision)


def _split_bf16(v):
    hi = v.astype(BF16)
    return hi, (v - hi.astype(F32)).astype(BF16)


def _mm_body(a_ref, w_ref, o_ref, *, tn, act, split):
    a = a_ref[...]
    if split:
        a_hi, a_lo = _split_bf16(a)
    else:
        a = a.astype(BF16)
    for n0 in range(0, o_ref.shape[1], tn):
        if split:
            w_hi, w_lo = _split_bf16(w_ref[:, n0:n0 + tn])
            acc = jnp.dot(a_hi, w_hi, preferred_element_type=F32) + (
                jnp.dot(a_hi, w_lo, preferred_element_type=F32) + jnp.dot(a_lo, w_hi, preferred_element_type=F32))
        else:
            acc = jnp.dot(a, w_ref[:, n0:n0 + tn], preferred_element_type=F32)
        if act == "gelu":
            acc = jax.nn.gelu(acc)
        o_ref[:, n0:n0 + tn] = acc.astype(o_ref.dtype)


def _matmul(a, w, *, tm, tn, out_dtype, act=None, split=False):
    m, k = a.shape
    n = w.shape[1]
    return pl.pallas_call(
        functools.partial(_mm_body, tn=tn, act=act, split=split),
        name="matmul",
        grid=(m // tm,),
        in_specs=[pl.BlockSpec((tm, k), lambda i: (i, 0)), pl.BlockSpec((k, n), lambda i: (0, 0))],
        out_specs=pl.BlockSpec((tm, n), lambda i: (i, 0)),
        out_shape=jax.ShapeDtypeStruct((m, n), out_dtype),
        compiler_params=_params("parallel"),
    )(a, w)


def _proj_ln_body(a_ref, w_ref, x_ref, g_ref, b_ref, o_ref):
    h = jnp.dot(a_ref[...].astype(BF16), w_ref[...], preferred_element_type=F32)
    o_ref[...] = _layer_norm(ALPHA * x_ref[...] + h, g_ref[...], b_ref[...])


def _proj_ln(a, w, x, g, b, *, tm=512):
    m, k = a.shape
    return pl.pallas_call(
        _proj_ln_body,
        name="proj_ln",
        grid=(m // tm,),
        in_specs=[pl.BlockSpec((tm, k), lambda i: (i, 0)), pl.BlockSpec((k, D_MODEL), lambda i: (0, 0)),
                  pl.BlockSpec((tm, D_MODEL), lambda i: (i, 0)), pl.BlockSpec((1, D_MODEL), lambda i: (0, 0)),
                  pl.BlockSpec((1, D_MODEL), lambda i: (0, 0))],
        out_specs=pl.BlockSpec((tm, D_MODEL), lambda i: (i, 0)),
        out_shape=jax.ShapeDtypeStruct((m, D_MODEL), F32),
        compiler_params=_params("parallel"),
    )(a, w, x, g, b)


def _ffn_body(x_ref, wg_ref, wu_ref, wd_ref, g_ref, b_ref, o_ref, acc_ref, xb_ref):
    j = pl.program_id(1)

    @pl.when(j == 0)
    def _():
        acc_ref[...] = jnp.zeros_like(acc_ref)
        xb_ref[...] = x_ref[...].astype(BF16)

    xb = xb_ref[...]
    gate = jnp.dot(xb, wg_ref[...].astype(BF16), preferred_element_type=F32)
    up = jnp.dot(xb, wu_ref[...].astype(BF16), preferred_element_type=F32)
    h = (gate * jax.nn.sigmoid(gate)) * up
    acc_ref[...] += jnp.dot(h.astype(BF16), wd_ref[...].astype(BF16), preferred_element_type=F32)

    @pl.when(j == pl.num_programs(1) - 1)
    def _():
        o_ref[...] = _layer_norm(ALPHA * x_ref[...] + acc_ref[...], g_ref[...], b_ref[...])


def _dense_ffn(x, w_gu, w_down, g, b, *, tm=1024, tf=512):
    m = x.shape[0]
    nf = FFN_DIM // tf
    return pl.pallas_call(
        _ffn_body,
        name="dense_ffn",
        grid=(m // tm, nf),
        in_specs=[pl.BlockSpec((tm, D_MODEL), lambda i, j: (i, 0)),
                  pl.BlockSpec((D_MODEL, tf), lambda i, j: (0, j)),
                  pl.BlockSpec((D_MODEL, tf), lambda i, j: (0, nf + j)),
                  pl.BlockSpec((tf, D_MODEL), lambda i, j: (j, 0)),
                  pl.BlockSpec((1, D_MODEL), lambda i, j: (0, 0)),
                  pl.BlockSpec((1, D_MODEL), lambda i, j: (0, 0))],
        out_specs=pl.BlockSpec((tm, D_MODEL), lambda i, j: (i, 0)),
        out_shape=jax.ShapeDtypeStruct((m, D_MODEL), F32),
        scratch_shapes=[pltpu.VMEM((tm, D_MODEL), F32), pltpu.VMEM((tm, D_MODEL), BF16)],
        compiler_params=_params("parallel", "arbitrary"),
    )(x, w_gu, w_gu, w_down, g, b)


ROW_TILE = (8, LANES)


def _to_row_tiles(dst_ref, val):
    for k in range(ROW_TILE[0]):
        dst_ref[:, k, :] = val[:, k * LANES:(k + 1) * LANES]


def _from_row_tiles(src_ref, r0, n):
    return jnp.concatenate([src_ref[r0:r0 + n, k, :] for k in range(ROW_TILE[0])], axis=1)


def _router_body(x_ref, wr_ref, e_ref, g_ref):
    logits = _nt_dot(wr_ref[...], x_ref[...], precision=HIGHEST)
    eid = lax.broadcasted_iota(I32, logits.shape, 0).astype(F32)
    m1 = jnp.max(logits, axis=0, keepdims=True)
    i1 = jnp.min(jnp.where(logits == m1, eid, float(N_EXPERTS)), axis=0, keepdims=True)
    rest = jnp.where(eid == i1, -jnp.inf, logits)
    m2 = jnp.max(rest, axis=0, keepdims=True)
    i2 = jnp.min(jnp.where(rest == m2, eid, float(N_EXPERTS)), axis=0, keepdims=True)
    e2 = jnp.exp(m2 - m1)
    den = 1.0 + e2
    e_ref[...] = jnp.concatenate([i1, i2], axis=0).astype(I32)
    g_ref[...] = jnp.concatenate([1.0 / den, e2 / den], axis=0)


def _router(x, w_router_t, *, tm=512):
    m = x.shape[0]
    return pl.pallas_call(
        _router_body,
        name="router",
        grid=(m // tm,),
        in_specs=[pl.BlockSpec((tm, D_MODEL), lambda i: (i, 0)), pl.BlockSpec((N_EXPERTS, D_MODEL), lambda i: (0, 0))],
        out_specs=[pl.BlockSpec((TOP_K, tm), lambda i: (0, i)), pl.BlockSpec((TOP_K, tm), lambda i: (0, i))],
        out_shape=[jax.ShapeDtypeStruct((TOP_K, m), I32), jax.ShapeDtypeStruct((TOP_K, m), F32)],
        compiler_params=_params("parallel"),
    )(x, w_router_t)


def _start_row_gather(idx_ref, src_hbm, dst_ref, sem):
    def start(r, c):
        pltpu.make_async_copy(src_hbm.at[pl.ds(idx_ref[0, 0, r], 1)], dst_ref.at[pl.ds(r, 1)], sem).start()
        return c

    lax.fori_loop(0, dst_ref.shape[0], start, 0, unroll=8)


def _wait_row_gather(src_hbm, dst_ref, sem):
    pltpu.make_async_copy(src_hbm.at[pl.ds(0, dst_ref.shape[0])], dst_ref, sem).wait()


def _moe_body(te_ref, nu_ref, tok_ref, nxt_ref, x_hbm, wg_ref, wu_ref, wd_ref, y_ref, xrow_ref, xb_ref, acc_ref, sem):
    del te_ref
    i = pl.program_id(0)
    j = pl.program_id(1)
    tm = xb_ref.shape[0]
    n_used = nu_ref[0]

    @pl.when((j == 0) & (i < n_used))
    def _():
        slot = lax.rem(i, 2)

        @pl.when(i == 0)
        def _():
            _start_row_gather(tok_ref, x_hbm, xrow_ref.at[0], sem.at[0])

        @pl.when(i + 1 < n_used)
        def _():
            _start_row_gather(nxt_ref, x_hbm, xrow_ref.at[1 - slot], sem.at[1 - slot])

        _wait_row_gather(x_hbm, xrow_ref.at[slot], sem.at[slot])
        xb_ref[...] = xrow_ref[slot].astype(BF16)
        acc_ref[...] = jnp.zeros_like(acc_ref)

    @pl.when(i < n_used)
    def _():
        xb = xb_ref[...]
        gate = jnp.dot(xb, wg_ref[...].astype(BF16), preferred_element_type=F32)
        up = jnp.dot(xb, wu_ref[...].astype(BF16), preferred_element_type=F32)
        h = (gate * jax.nn.sigmoid(gate)) * up
        acc_ref[...] += jnp.dot(h.astype(BF16), wd_ref[...].astype(BF16), preferred_element_type=F32)

    @pl.when(j == pl.num_programs(1) - 1)
    def _():
        @pl.when(i < n_used)
        def _():
            _to_row_tiles(y_ref, acc_ref[...])

        @pl.when(i >= n_used)
        def _():
            y_ref[...] = jnp.zeros_like(y_ref)


def _combine_body(pos_ref, nxt_ref, x_ref, gt_ref, y_hbm, g_ref, b_ref, o_ref, rows_ref, sem):
    i = pl.program_id(0)
    tq = x_ref.shape[0]
    slot = lax.rem(i, 2)

    @pl.when(i == 0)
    def _():
        _start_row_gather(pos_ref, y_hbm, rows_ref.at[0], sem.at[0])

    @pl.when(i + 1 < pl.num_programs(0))
    def _():
        _start_row_gather(nxt_ref, y_hbm, rows_ref.at[1 - slot], sem.at[1 - slot])

    _wait_row_gather(y_hbm, rows_ref.at[slot], sem.at[slot])
    rows = rows_ref.at[slot]
    gt = gt_ref[...]
    h = gt[:, 0:1] * _from_row_tiles(rows, 0, tq) + gt[:, 1:2] * _from_row_tiles(rows, tq, tq)
    o_ref[...] = _layer_norm(ALPHA * x_ref[...] + h, g_ref[...], b_ref[...])


def _moe_ffn(x, w_router_t, w_gu, w_down, g, b, *, tm=1024, tf=512, tq=256):
    m = x.shape[0]
    n_assign = m * TOP_K
    n_tiles = n_assign // tm + N_EXPERTS
    n_rows = n_tiles * tm
    nf = FFN_DIM // tf

    eidx, gate = _router(x, w_router_t)
    flat_e = eidx.reshape(-1)
    onehot = (flat_e[:, None] == jnp.arange(N_EXPERTS, dtype=I32)[None, :]).astype(I32)
    csum = jnp.cumsum(onehot, axis=0)
    rank = jnp.sum(onehot * csum, axis=1) - 1
    counts = csum[-1]
    padded = (counts + tm - 1) // tm * tm
    padded_ends = jnp.cumsum(padded)
    padded_starts = padded_ends - padded
    dest = padded_starts[flat_e] + rank
    tok = jnp.tile(jnp.arange(m, dtype=I32), TOP_K)
    row_tok3 = jnp.zeros((n_rows,), I32).at[dest].set(tok, unique_indices=True).reshape(n_tiles, 1, tm)
    n_used = (padded_ends[-1:] // tm).astype(I32)
    w_step = lambda i, j, nu: jnp.where(i < nu[0], j, nf - 1)
    tile_start = jnp.arange(n_tiles, dtype=I32) * tm
    tile_e = jnp.minimum(jnp.sum((tile_start[:, None] >= padded_ends[None, :]).astype(I32), axis=1), N_EXPERTS - 1)

    y = pl.pallas_call(
        _moe_body,
        name="moe_ffn",
        grid_spec=pltpu.PrefetchScalarGridSpec(
            num_scalar_prefetch=2,
            grid=(n_tiles, nf),
            in_specs=[pl.BlockSpec((1, 1, tm), lambda i, j, te, nu: (i, 0, 0), memory_space=pltpu.SMEM),
                      pl.BlockSpec((1, 1, tm), lambda i, j, te, nu: (jnp.minimum(i + 1, n_tiles - 1), 0, 0),
                                   memory_space=pltpu.SMEM),
                      pl.BlockSpec(memory_space=pl.ANY),
                      pl.BlockSpec((None, D_MODEL, tf), lambda i, j, te, nu: (te[i], 0, w_step(i, j, nu))),
                      pl.BlockSpec((None, D_MODEL, tf), lambda i, j, te, nu: (te[i], 0, nf + w_step(i, j, nu))),
                      pl.BlockSpec((None, tf, D_MODEL), lambda i, j, te, nu: (te[i], w_step(i, j, nu), 0))],
            out_specs=pl.BlockSpec((tm,) + ROW_TILE, lambda i, j, te, nu: (i, 0, 0)),
            scratch_shapes=[pltpu.VMEM((2, tm, D_MODEL), F32), pltpu.VMEM((tm, D_MODEL), BF16),
                            pltpu.VMEM((tm, D_MODEL), F32), pltpu.SemaphoreType.DMA((2,))]),
        out_shape=jax.ShapeDtypeStruct((n_rows,) + ROW_TILE, F32),
        compiler_params=_params("arbitrary", "arbitrary"),
    )(tile_e, n_used, row_tok3, row_tok3, x, w_gu, w_gu, w_down)

    pos = jnp.concatenate([dest[:m].reshape(m // tq, 1, tq), dest[m:].reshape(m // tq, 1, tq)], axis=2)
    return pl.pallas_call(
        _combine_body,
        name="moe_combine",
        grid=(m // tq,),
        in_specs=[pl.BlockSpec((1, 1, 2 * tq), lambda i: (i, 0, 0), memory_space=pltpu.SMEM),
                  pl.BlockSpec((1, 1, 2 * tq), lambda i: (jnp.minimum(i + 1, m // tq - 1), 0, 0),
                               memory_space=pltpu.SMEM),
                  pl.BlockSpec((tq, D_MODEL), lambda i: (i, 0)),
                  pl.BlockSpec((tq, TOP_K), lambda i: (i, 0)),
                  pl.BlockSpec(memory_space=pl.ANY),
                  pl.BlockSpec((1, D_MODEL), lambda i: (0, 0)),
                  pl.BlockSpec((1, D_MODEL), lambda i: (0, 0))],
        out_specs=pl.BlockSpec((tq, D_MODEL), lambda i: (i, 0)),
        out_shape=jax.ShapeDtypeStruct((m, D_MODEL), F32),
        scratch_shapes=[pltpu.VMEM((2, 2 * tq) + ROW_TILE, F32), pltpu.SemaphoreType.DMA((2,))],
        compiler_params=_params("arbitrary"),
    )(pos, pos, x, gate.T, y, g, b)


def _gmlp_gate_body(u_ref, v_ref, lg_ref, lb_ref, ws_ref, bs_ref, o_ref):
    tm = u_ref.shape[0]
    v = _layer_norm(v_ref[...].astype(F32), lg_ref[...], lb_ref[...]).astype(BF16)
    r = lax.broadcasted_iota(I32, (A_CHUNK, A_CHUNK), 0)
    c = lax.broadcasted_iota(I32, (A_CHUNK, A_CHUNK), 1)
    causal = c <= r
    bs = bs_ref[...]
    for grp in range(A_GROUPS):
        wc = jnp.where(causal, ws_ref[grp], 0.0).astype(BF16)
        lo = grp * A_GROUP_DIM
        for ch in range(tm // A_CHUNK):
            r0 = ch * A_CHUNK
            s = jnp.dot(wc, v[r0:r0 + A_CHUNK, lo:lo + A_GROUP_DIM], preferred_element_type=F32)
            s = s + bs[:, grp:grp + 1]
            u = u_ref[r0:r0 + A_CHUNK, lo:lo + A_GROUP_DIM].astype(F32)
            o_ref[r0:r0 + A_CHUNK, lo:lo + A_GROUP_DIM] = (u * s).astype(BF16)


def _gmlp_mixer(x, w_in, ln_g, ln_b, w_s, b_s_t, w_out, g, b, *, tm=256):
    m = x.shape[0]
    uv = _matmul(x, w_in, tm=512, tn=1024, out_dtype=BF16, act="gelu")
    gated = pl.pallas_call(
        _gmlp_gate_body,
        name="gmlp_gate",
        grid=(m // tm,),
        in_specs=[pl.BlockSpec((tm, A_HALF), lambda i: (i, 0)), pl.BlockSpec((tm, A_HALF), lambda i: (i, 1)),
                  pl.BlockSpec((1, A_HALF), lambda i: (0, 0)), pl.BlockSpec((1, A_HALF), lambda i: (0, 0)),
                  pl.BlockSpec((A_GROUPS, A_CHUNK, A_CHUNK), lambda i: (0, 0, 0)),
                  pl.BlockSpec((A_CHUNK, A_GROUPS), lambda i: (0, 0))],
        out_specs=pl.BlockSpec((tm, A_HALF), lambda i: (i, 0)),
        out_shape=jax.ShapeDtypeStruct((m, A_HALF), BF16),
        compiler_params=_params("parallel"),
    )(uv, uv, ln_g, ln_b, w_s, b_s_t)
    return _proj_ln(gated, w_out, x, g, b)


def _flash_pairs(pairs, tq, n_plain, n_kc, ck, bias_fn):
    even = lax.broadcasted_iota(I32, (tq, LANES), 1) < (LANES // 2)
    ones = jnp.ones((ck, LANES), BF16)

    def head(q_ref, q_lane, k_ref, k_lane, c0, bias, v_aug, m):
        s = _nt_dot(q_ref[:, q_lane:q_lane + LANES], k_ref[pl.ds(c0, ck), k_lane:k_lane + LANES])
        if bias is not None:
            s = s + bias
        m_new = jnp.maximum(m, jnp.max(s, axis=1, keepdims=True))
        p = jnp.exp2(s - m_new).astype(BF16)
        pv = jnp.dot(p, v_aug, preferred_element_type=F32)
        return m_new, jnp.exp2(m - m_new), pv

    def step(c, carry, masked):
        c0 = pl.multiple_of(c * ck, ck)
        bias = bias_fn(c0) if masked else None
        new = []
        for (qe_ref, qe_lane, qo_ref, qo_lane, k_ref, kle, klo, v_ref, v_lane), (me, mo, l, acc) in zip(pairs, carry):
            v_aug = jnp.concatenate([v_ref[pl.ds(c0, ck), v_lane:v_lane + LANES], ones], axis=1)
            me, ae, pve = head(qe_ref, qe_lane, k_ref, kle, c0, bias, v_aug, me)
            mo, ao, pvo = head(qo_ref, qo_lane, k_ref, klo, c0, bias, v_aug, mo)
            a = jnp.where(even, ae, ao)
            l = a * l + jnp.where(even, pve[:, LANES:], pvo[:, LANES:])
            acc = a * acc + jnp.where(even, pve[:, :LANES], pvo[:, :LANES])
            new.append((me, mo, l, acc))
        return tuple(new)

    stat = jnp.full((tq, 1), NEG, F32)
    init = tuple((stat, stat, jnp.zeros((tq, LANES), F32), jnp.zeros((tq, LANES), F32)) for _ in pairs)
    carry = lax.fori_loop(0, n_plain, lambda c, cr: step(c, cr, False), init)
    carry = lax.fori_loop(n_plain, n_kc, lambda c, cr: step(c, cr, True), carry)
    return [acc / l for (_, _, l, acc) in carry]


def _dsa_body(q_ref, k_ref, v_ref, qi_ref, ki_ref, wt_ref, o_ref, s_ref, bias_ref, qe_ref, qo_ref, qih_ref, qil_ref,
              jcut_ref, *, tq, ck, topk):
    i = pl.program_id(1)
    sc = 2 * LANES
    n_sc = (i * tq + tq) // sc
    n_kc = (i * tq + tq + ck - 1) // ck
    seq = s_ref.shape[0]
    kpos = lax.broadcasted_iota(I32, (sc, tq), 0)
    qpos = i * tq + lax.broadcasted_iota(I32, (sc, tq), 1)
    lane = lax.broadcasted_iota(I32, (tq, LANES), 1)
    idx_scale = (B_IDX_DIM * B_IDX_HEADS) ** -0.5

    for h in range(B_IDX_HEADS):
        q2 = qi_ref[:, (h // 2) * LANES:(h // 2 + 1) * LANES]
        in_head = (lane >= B_IDX_DIM) if h % 2 else (lane < B_IDX_DIM)
        hi, lo = _split_bf16(jnp.where(in_head, q2, 0.0))
        qih_ref[:, h * LANES:(h + 1) * LANES] = hi
        qil_ref[:, h * LANES:(h + 1) * LANES] = lo

    def score_chunk(c, carry):
        c0 = pl.multiple_of(c * sc, sc)

        @pl.when(c < n_sc)
        def _():
            k_hi, k_lo = _split_bf16(ki_ref[pl.ds(c0, sc), :])
            score = jnp.zeros((sc, tq), F32)
            for h in range(B_IDX_HEADS):
                q_hi = qih_ref[:, h * LANES:(h + 1) * LANES]
                dots = _nt_dot(k_hi, q_hi) + (_nt_dot(k_hi, qil_ref[:, h * LANES:(h + 1) * LANES]) + _nt_dot(k_lo, q_hi))
                score = score + jnp.maximum(dots, 0.0) * wt_ref[h:h + 1, :]
            score = score * idx_scale
            score = jnp.where(score == 0.0, 0.0, score)
            s_ref[pl.ds(c0, sc), :] = jnp.where(c0 + kpos <= qpos, score, -jnp.inf)

        @pl.when(c >= n_sc)
        def _():
            s_ref[pl.ds(c0, sc), :] = jnp.full((sc, tq), -jnp.inf, F32)

        return carry

    lax.fori_loop(0, n_kc * (ck // sc), score_chunk, 0)

    def count(pred):
        def cb(c, acc):
            c0 = pl.multiple_of(c * sc, sc)
            hit = jnp.where(pred(s_ref[pl.ds(c0, sc), :], c0 + kpos), 1.0, 0.0)
            return acc + jnp.sum(hit.reshape(sc // 8, 8, tq), axis=0)

        return jnp.sum(lax.fori_loop(0, n_sc, cb, jnp.zeros((8, tq), F32)), axis=0, keepdims=True)

    def key_to_f32(key):
        return lax.bitcast_convert_type(key ^ (lax.shift_right_arithmetic(key, 31) & jnp.int32(0x7FFFFFFF)), F32)

    def bit_step(bi, ans):
        cand = ans + lax.shift_left(jnp.int32(1), 31 - bi)
        cand_f = key_to_f32(cand)
        return jnp.where(count(lambda sv, kp: sv >= cand_f) >= float(topk), cand, ans)

    thr = key_to_f32(lax.fori_loop(0, 32, bit_step, jnp.full((1, tq), INT_MIN, I32)))
    few = qpos[0:1, :] < topk

    n_ge = count(lambda sv, kp: sv >= thr)
    need = float(topk) - count(lambda sv, kp: sv > thr)
    jcut_ref[...] = jnp.full(jcut_ref.shape, seq, I32)
    surplus = jnp.where((n_ge > float(topk)) & jnp.logical_not(few), 1.0, 0.0)

    @pl.when(jnp.max(surplus) > 0.0)
    def _():
        nbits = (seq - 1).bit_length()

        def idx_step(bi, ans):
            cand = ans + lax.shift_left(jnp.int32(1), nbits - 1 - bi)
            return jnp.where(count(lambda sv, kp: (sv == thr) & (kp < cand)) < need, cand, ans)

        jcut_ref[...] = jnp.broadcast_to(lax.fori_loop(0, nbits, idx_step, jnp.zeros((1, tq), I32)), jcut_ref.shape)

    thr_sel = jnp.where(few, -jnp.inf, thr)
    jcut = jnp.where(few, -1, jcut_ref[0:1, :])

    def bias_chunk(c, carry):
        c0 = pl.multiple_of(c * sc, sc)
        sv = s_ref[pl.ds(c0, sc), :]
        sel = (sv > thr_sel) | ((sv == thr_sel) & (c0 + kpos <= jcut))
        for r0 in range(0, tq, sc):
            bias_ref[r0:r0 + sc, pl.ds(c0, sc)] = jnp.where(sel, 0.0, NEG)[:, r0:r0 + sc].T
        return carry

    lax.fori_loop(0, n_kc * (ck // sc), bias_chunk, 0)

    q_scale = B_HEAD_DIM ** -0.5 * LOG2E
    for p in range(B_HEADS // 2):
        ls = slice(p * LANES, (p + 1) * LANES)
        q2 = q_ref[:, ls].astype(F32) * q_scale
        qe_ref[:, ls] = jnp.where(lane < B_HEAD_DIM, q2, 0.0).astype(BF16)
        qo_ref[:, ls] = jnp.where(lane >= B_HEAD_DIM, q2, 0.0).astype(BF16)
    bias_fn = lambda c0: bias_ref[:, pl.ds(c0, ck)]
    group = 4
    for p0 in range(0, B_HEADS // 2, group):
        pairs = [(qe_ref, p * LANES, qo_ref, p * LANES, k_ref, p * LANES, p * LANES, v_ref, p * LANES)
                 for p in range(p0, p0 + group)]
        for p, o in zip(range(p0, p0 + group), _flash_pairs(pairs, tq, 0, n_kc, ck, bias_fn)):
            o_ref[:, p * LANES:(p + 1) * LANES] = o.astype(BF16)


def _dsa_mixer(x, w_qkv, w_idx, w_out, g, b, *, batch, seq, tq=256, ck=512):
    m = x.shape[0]
    nq = seq // tq
    hd = B_HEADS * B_HEAD_DIM
    topk = min(B_TOPK_MAX, seq // 4)
    qkv = _matmul(x, w_qkv, tm=512, tn=1024, out_dtype=BF16)
    idx = _matmul(x, w_idx, tm=512, tn=512, out_dtype=F32, split=True)
    w_rows = jnp.pad(idx[:, 3 * LANES:3 * LANES + B_IDX_HEADS].T, ((0, 8 - B_IDX_HEADS), (0, 0)))
    attn = pl.pallas_call(
        functools.partial(_dsa_body, tq=tq, ck=ck, topk=topk),
        name="dsa_attn",
        grid=(batch, nq),
        in_specs=[pl.BlockSpec((tq, hd), lambda bi, i: (bi * nq + i, 0)),
                  pl.BlockSpec((seq, hd), lambda bi, i: (bi, 1)),
                  pl.BlockSpec((seq, hd), lambda bi, i: (bi, 2)),
                  pl.BlockSpec((tq, 2 * LANES), lambda bi, i: (bi * nq + i, 0)),
                  pl.BlockSpec((seq, LANES), lambda bi, i: (bi, 2)),
                  pl.BlockSpec((8, tq), lambda bi, i: (0, bi * nq + i))],
        out_specs=pl.BlockSpec((tq, hd), lambda bi, i: (bi * nq + i, 0)),
        out_shape=jax.ShapeDtypeStruct((m, hd), BF16),
        scratch_shapes=[pltpu.VMEM((seq, tq), F32), pltpu.VMEM((tq, seq), F32), pltpu.VMEM((tq, hd), BF16),
                        pltpu.VMEM((tq, hd), BF16), pltpu.VMEM((tq, B_IDX_HEADS * LANES), BF16),
                        pltpu.VMEM((tq, B_IDX_HEADS * LANES), BF16), pltpu.VMEM((8, tq), I32)],
        compiler_params=_params("parallel", "arbitrary"),
    )(qkv, qkv, qkv, idx, idx, w_rows)
    return _proj_ln(attn, w_out, x, g, b)


def _hgrn2_body(q_ref, f_ref, i_ref, gt_ref, lbl_ref, ng_ref, o_ref, st_ref, *, layer):
    @pl.when(pl.program_id(2) == 0)
    def _():
        st_ref[...] = jnp.zeros_like(st_ref)

    lg = lbl_ref[...]
    e = jnp.exp(lg - jnp.max(lg, axis=0, keepdims=True))
    sm = e / jnp.sum(e, axis=0, keepdims=True)
    lb_all = [sm[0:1]]
    for d in range(1, DEPTH):
        lb_all.append(lb_all[-1] + sm[d:d + 1])
    lb = lb_all[layer] - lb_all[0]
    ng = ng_ref[...]
    rid = lax.broadcasted_iota(I32, (C_SUB, LANES), 0)
    rid8 = lax.broadcasted_iota(I32, (C_SUB // 2, LANES), 0)

    def sub_chunk(j, carry):
        r0 = pl.multiple_of(j * C_SUB, C_SUB)
        for h in range(C_HG):
            ls = slice(h * LANES, (h + 1) * LANES)
            qb = q_ref[pl.ds(r0, C_SUB), ls]
            lbh = lb[:, ls]
            f = lbh + (1.0 - lbh) * jax.nn.sigmoid(f_ref[pl.ds(r0, C_SUB), ls])
            gb = jnp.log(f)
            kb = 1.0 - f
            vb = i_ref[pl.ds(r0, C_SUB), ls]
            gc = gb
            for sh in (1, 2, 4, 8):
                gc = gc + jnp.where(rid >= sh, pltpu.roll(gc, sh, 0), 0.0)
            glast = gc[C_SUB - 1:C_SUB, :]
            st = st_ref[h]
            out = _nt_dot((qb * jnp.exp(gc)).astype(BF16), st.astype(BF16))
            half = C_SUB // 2
            intra = []
            for blk in range(2):
                rows = slice(blk * half, (blk + 1) * half)
                q_b, g_b = qb[rows], gc[rows]
                acc = jnp.zeros((half, LANES), F32)
                for s in range((blk + 1) * half):
                    w = q_b * kb[s:s + 1, :]
                    if s >= blk * half:
                        keep = rid8 >= s - blk * half
                        w = jnp.where(keep, w * jnp.exp(jnp.where(keep, g_b - gc[s:s + 1, :], 0.0)), 0.0)
                    else:
                        w = w * jnp.exp(g_b - gc[s:s + 1, :])
                    acc = acc + jnp.sum(w, axis=1, keepdims=True) * vb[s:s + 1, :]
                intra.append(acc)
            out = out + jnp.concatenate(intra, axis=0)
            kdec = kb * jnp.exp(glast - gc)
            kv_t = lax.dot_general(vb.astype(BF16), kdec.astype(BF16), (((0,), (0,)), ((), ())),
                                   preferred_element_type=F32)
            st_ref[h] = st * jnp.exp(glast) + kv_t
            gate = gt_ref[pl.ds(r0, C_SUB), ls]
            o = out * lax.rsqrt(jnp.mean(out * out, axis=1, keepdims=True) + RMS_EPS) * ng
            o_ref[pl.ds(r0, C_SUB), ls] = (o * (gate * jax.nn.sigmoid(gate))).astype(BF16)
        return carry

    lax.fori_loop(0, q_ref.shape[0] // C_SUB, sub_chunk, 0)


def _hgrn2_mixer(x, w_in, lb_logits, norm_g, w_out, g, b, *, batch, seq, layer):
    m = x.shape[0]
    ts = min(C_TS, seq)
    ns = seq // ts
    w = C_HG * LANES
    nhb = C_HEADS // C_HG
    proj = _matmul(x, w_in, tm=512, tn=1024, out_dtype=F32)
    spec = lambda blk: pl.BlockSpec((ts, w), lambda bi, hb, s: (bi * ns + s, blk * nhb + hb))
    o = pl.pallas_call(
        functools.partial(_hgrn2_body, layer=layer),
        name="hgrn2_scan",
        grid=(batch, nhb, ns),
        in_specs=[spec(0), spec(1), spec(2), spec(3),
                  pl.BlockSpec((DEPTH, w), lambda bi, hb, s: (0, hb)),
                  pl.BlockSpec((1, LANES), lambda bi, hb, s: (0, 0))],
        out_specs=pl.BlockSpec((ts, w), lambda bi, hb, s: (bi * ns + s, hb)),
        out_shape=jax.ShapeDtypeStruct((m, D_MODEL), BF16),
        scratch_shapes=[pltpu.VMEM((C_HG, C_HEAD_V, C_EXPAND), F32)],
        compiler_params=_params("parallel", "parallel", "arbitrary"),
    )(proj, proj, proj, proj, lb_logits, norm_g)
    return _proj_ln(o, w_out, x, g, b)


def _mla_prep_body(p_ref, pos_ref, invf_ref, qg_ref, kvg_ref, wq_ref, wk_ref, wv_ref, q_ref, k_ref, v_ref):
    tm = p_ref.shape[0]
    lane = lax.broadcasted_iota(I32, (tm, LANES), 1)
    ang = pos_ref[...].astype(F32) * invf_ref[...]
    cos_t = jnp.cos(ang)
    sin_t = jnp.sin(ang)
    cq = p_ref[:, 0:D_Q_RANK]
    cq = cq * lax.rsqrt(jnp.mean(cq * cq, axis=1, keepdims=True) + RMS_EPS) * qg_ref[...]
    ckv = p_ref[:, D_Q_RANK:D_Q_RANK + D_KV_RANK]
    ckv = ckv * lax.rsqrt(jnp.mean(ckv * ckv, axis=1, keepdims=True) + RMS_EPS) * kvg_ref[...]
    k_rot = p_ref[:, 3 * LANES:4 * LANES] * cos_t + p_ref[:, 4 * LANES:5 * LANES] * sin_t
    q_tab = jnp.where(lane < D_NOPE + D_ROPE, cos_t, sin_t) * ((D_NOPE + D_ROPE) ** -0.5 * LOG2E)
    cqb = cq.astype(BF16)
    ckvb = ckv.astype(BF16)
    for h in range(D_HEADS):
        ls = slice(h * LANES, (h + 1) * LANES)
        qh = jnp.dot(cqb, wq_ref[:, ls], preferred_element_type=F32)
        q_ref[:, ls] = (qh * q_tab).astype(BF16)
        kh = jnp.dot(ckvb, wk_ref[:, ls], preferred_element_type=F32)
        k_ref[:, ls] = (kh + k_rot).astype(BF16)
    v_ref[...] = jnp.dot(ckvb, wv_ref[...], preferred_element_type=F32).astype(BF16)


def _mla_attn_body(q_ref, k_ref, v_ref, o_ref, *, tq, ck):
    i = pl.program_id(2)
    n_kc = (i * tq + tq + ck - 1) // ck
    n_plain = (i * tq + 1) // ck
    qpos = i * tq + lax.broadcasted_iota(I32, (tq, ck), 0)
    kpos = lax.broadcasted_iota(I32, (tq, ck), 1)
    bias_fn = lambda c0: jnp.where(c0 + kpos <= qpos, 0.0, NEG)
    pairs = [(q_ref, 2 * p * LANES, q_ref, (2 * p + 1) * LANES, k_ref, 2 * p * LANES, (2 * p + 1) * LANES, v_ref, p * LANES)
             for p in range(D_PAIRS)]
    for p, o in enumerate(_flash_pairs(pairs, tq, n_plain, n_kc, ck, bias_fn)):
        o_ref[:, p * LANES:(p + 1) * LANES] = o.astype(BF16)


def _mla_mixer(x, positions, w_in, invf, q_norm_g, kv_norm_g, w_q, w_k, w_v, w_out, g, b, *, batch, seq, tm=256, tq=256, ck=512):
    m = x.shape[0]
    nq = seq // tq
    hw = D_HEADS * LANES
    proj = _matmul(x, w_in, tm=512, tn=w_in.shape[1], out_dtype=F32)
    hv = D_HEADS * D_VDIM
    q, k, v = pl.pallas_call(
        _mla_prep_body,
        name="mla_prep",
        grid=(m // tm,),
        in_specs=[pl.BlockSpec((tm, w_in.shape[1]), lambda i: (i, 0)), pl.BlockSpec((tm, 1), lambda i: (i, 0)),
                  pl.BlockSpec((1, LANES), lambda i: (0, 0)), pl.BlockSpec((1, D_Q_RANK), lambda i: (0, 0)),
                  pl.BlockSpec((1, D_KV_RANK), lambda i: (0, 0)), pl.BlockSpec((D_Q_RANK, hw), lambda i: (0, 0)),
                  pl.BlockSpec((D_KV_RANK, hw), lambda i: (0, 0)),
                  pl.BlockSpec((D_KV_RANK, hv), lambda i: (0, 0))],
        out_specs=[pl.BlockSpec((tm, hw), lambda i: (i, 0)), pl.BlockSpec((tm, hw), lambda i: (i, 0)),
                   pl.BlockSpec((tm, hv), lambda i: (i, 0))],
        out_shape=[jax.ShapeDtypeStruct((m, hw), BF16), jax.ShapeDtypeStruct((m, hw), BF16),
                   jax.ShapeDtypeStruct((m, hv), BF16)],
        compiler_params=_params("parallel"),
    )(proj, positions.reshape(m, 1), invf, q_norm_g, kv_norm_g, w_q, w_k, w_v)
    attn = pl.pallas_call(
        functools.partial(_mla_attn_body, tq=tq, ck=ck),
        name="mla_attn",
        grid=(batch, D_HEADS // (2 * D_PAIRS), nq),
        in_specs=[pl.BlockSpec((tq, 2 * D_PAIRS * LANES), lambda bi, hp, i: (bi * nq + i, hp)),
                  pl.BlockSpec((seq, 2 * D_PAIRS * LANES), lambda bi, hp, i: (bi, hp)),
                  pl.BlockSpec((seq, D_PAIRS * LANES), lambda bi, hp, i: (bi, hp))],
        out_specs=pl.BlockSpec((tq, D_PAIRS * LANES), lambda bi, hp, i: (bi * nq + i, hp)),
        out_shape=jax.ShapeDtypeStruct((m, hv), BF16),
        compiler_params=_params("parallel", "parallel", "arbitrary"),
    )(q, k, v)
    return _proj_ln(attn, w_out, x, g, b)


def _rotate_half_cols(w):
    half = w.shape[-1] // 2
    return jnp.concatenate([-w[..., half:], w[..., :half]], axis=-1)


def _mla_weights(d_w_in, d_w_uq, d_w_ukv):
    zeros = lambda *s: jnp.zeros(s, F32)
    w_kr = d_w_in[:, D_Q_RANK + D_KV_RANK:]
    w_krh = _rotate_half_cols(w_kr)
    z64 = zeros(D_MODEL, D_NOPE)
    w_in = jnp.concatenate([d_w_in[:, :D_Q_RANK + D_KV_RANK], z64, w_kr, w_kr, z64, w_krh, w_krh], axis=1)
    uq = d_w_uq.reshape(D_Q_RANK, D_HEADS, D_NOPE + D_ROPE)
    w_q = jnp.concatenate([uq, _rotate_half_cols(uq[..., D_NOPE:])], axis=-1).reshape(D_Q_RANK, D_HEADS * LANES)
    ukv = d_w_ukv.reshape(D_KV_RANK, D_HEADS, D_NOPE + D_VDIM)
    w_k = jnp.concatenate([ukv[..., :D_NOPE], zeros(D_KV_RANK, D_HEADS, LANES - D_NOPE)], axis=-1)
    w_k = w_k.reshape(D_KV_RANK, D_HEADS * LANES)
    w_v = ukv[..., D_NOPE:].reshape(D_KV_RANK, D_HEADS * D_VDIM)
    half = D_ROPE // 2
    inv_freq = ROPE_BASE ** (-jnp.arange(half, dtype=F32) / half)
    invf = jnp.concatenate([jnp.zeros((D_NOPE,), F32), jnp.tile(inv_freq, 4)]).reshape(1, LANES)
    return w_in.astype(BF16), w_q.astype(BF16), w_k.astype(BF16), w_v.astype(BF16), invf


def _dsa_weights(b_w_in):
    hd = B_HEADS * B_HEAD_DIM
    o = 3 * hd
    nqi = B_IDX_HEADS * B_IDX_DIM
    w_qi = b_w_in[:, o:o + nqi]
    w_ki = b_w_in[:, o + nqi:o + nqi + B_IDX_DIM]
    w_wi = b_w_in[:, o + nqi + B_IDX_DIM:]
    pad = jnp.zeros((D_MODEL, LANES - B_IDX_HEADS), F32)
    w_idx = jnp.concatenate([w_qi, w_ki, w_ki, w_wi, pad], axis=1)
    return b_w_in[:, :o].astype(BF16), w_idx


def kernel(x, positions, a_w_in, a_ln_g, a_ln_b, a_w_s, a_b_s, a_w_out, b_w_in, b_w_out, c_w_in, c_lb_logits, c_norm_g, c_w_out, d_w_in, d_q_norm_g, d_w_uq, d_kv_norm_g, d_w_ukv, d_w_out, ffn0_w_gu, ffn0_w_down, moe1_w_router, moe1_w_gu, moe1_w_down, ffn2_w_gu, ffn2_w_down, moe3_w_router, moe3_w_gu, moe3_w_down, ln_mix_g, ln_mix_b, ln_ffn_g, ln_ffn_b):
    batch, seq, d = x.shape
    m = batch * seq
    row = lambda v: v.reshape(1, -1).astype(F32)
    bf = lambda w: w.astype(BF16)
    h = x.reshape(m, d)

    h = _gmlp_mixer(h, bf(a_w_in), row(a_ln_g), row(a_ln_b), a_w_s, a_b_s.T, bf(a_w_out),
                    row(ln_mix_g[0]), row(ln_mix_b[0]))
    h = _dense_ffn(h, ffn0_w_gu, ffn0_w_down, row(ln_ffn_g[0]), row(ln_ffn_b[0]))

    w_qkv, w_idx = _dsa_weights(b_w_in)
    h = _dsa_mixer(h, w_qkv, w_idx, bf(b_w_out), row(ln_mix_g[1]), row(ln_mix_b[1]), batch=batch, seq=seq)
    h = _moe_ffn(h, moe1_w_router.T, moe1_w_gu, moe1_w_down, row(ln_ffn_g[1]), row(ln_ffn_b[1]))

    h = _hgrn2_mixer(h, bf(c_w_in), c_lb_logits, row(c_norm_g), bf(c_w_out), row(ln_mix_g[2]), row(ln_mix_b[2]),
                     batch=batch, seq=seq, layer=2)
    h = _dense_ffn(h, ffn2_w_gu, ffn2_w_down, row(ln_ffn_g[2]), row(ln_ffn_b[2]))

    w_in, w_q, w_k, w_v, invf = _mla_weights(d_w_in, d_w_uq, d_w_ukv)
    h = _mla_mixer(h, positions, w_in, invf, row(d_q_norm_g), row(d_kv_norm_g), w_q, w_k, w_v, bf(d_w_out),
                   row(ln_mix_g[3]), row(ln_mix_b[3]), batch=batch, seq=seq)
    h = _moe_ffn(h, moe3_w_router.T, moe3_w_gu, moe3_w_down, row(ln_ffn_g[3]), row(ln_ffn_b[3]))
    return h.reshape(batch, seq, d)
```
